```python
import math
import jax, jax.numpy as jnp
from jax import lax
import numpy as np

D_MODEL = 2048
BATCH = 8
SEQ = 4096
DEPTH = 2

D_MIX = D_MODEL
D_A = D_MIX // 2
D_B = D_MIX - D_A
A_GROUPS = 16
B_HEADS = 16
B_HEAD_DIM = D_B // B_HEADS
CONV_A_WIDTH = 3
CONV_B_WIDTH = 4
LRU_C = 8.0
D_IN_EVEN = 3 * D_A + 2 * D_B
SB_HEADS = 16
SB_HEAD_DIM = D_MODEL // SB_HEADS
Q_BLOCK = 128
D_FF = 4 * D_MODEL
NORM_EPS = 1e-6
N_EVEN = (DEPTH + 1) // 2
N_ODD = DEPTH // 2

kernel_name = "hybrid_conv_rglru_stickbreak_block"


def rms_norm(x, g):
    xf = x.astype(jnp.float32)
    y = xf * lax.rsqrt(jnp.mean(xf * xf, axis=-1, keepdims=True) + NORM_EPS)
    return (y * g.astype(jnp.float32)).astype(x.dtype)


def causal_dwconv(x, w, bias=None):
    k_width = w.shape[0]
    s = x.shape[1]
    xp = jnp.pad(x, ((0, 0), (k_width - 1, 0), (0, 0)))
    y = w[k_width - 1] * x
    for k in range(k_width - 1):
        y = y + w[k] * xp[:, k:k + s]
    if bias is not None:
        y = y + bias
    return y


def rg_lru(x, w_a, b_a, w_x, b_x, lam):
    bsz, s, _ = x.shape
    xf = x.astype(jnp.float32)
    xh = xf.reshape(bsz, s, B_HEADS, B_HEAD_DIM)
    r = jax.nn.sigmoid(jnp.einsum('bshi,hij->bshj', xh, w_a.astype(jnp.float32)).reshape(bsz, s, D_B)
                       + b_a.astype(jnp.float32))
    i = jax.nn.sigmoid(jnp.einsum('bshi,hij->bshj', xh, w_x.astype(jnp.float32)).reshape(bsz, s, D_B)
                       + b_x.astype(jnp.float32))
    log_a = LRU_C * r * jax.nn.log_sigmoid(lam.astype(jnp.float32))
    a = jnp.exp(log_a)
    mult = jnp.sqrt(-jnp.expm1(2.0 * log_a))
    b = mult * (i * xf)

    def combine(e1, e2):
        a1, b1 = e1
        a2, b2 = e2
        return a1 * a2, a2 * b1 + b2

    _, h = lax.associative_scan(combine, (a, b), axis=1)
    return h.astype(x.dtype)


def conv_lru_mixer(h, w_in, conv_a, conv_b, conv_b_bias, rg_w_a, rg_b_a, rg_w_x, rg_b_x, rg_lambda, w_out):
    proj = h @ w_in
    a_bgate, a_cgate, a_x, b_gate, b_x = jnp.split(
        proj, [D_A, 2 * D_A, 3 * D_A, 3 * D_A + D_B], axis=-1)
    y_a = a_bgate * causal_dwconv(a_cgate * a_x, conv_a)
    xr = causal_dwconv(b_x, conv_b, conv_b_bias)
    y_b = rg_lru(xr, rg_w_a, rg_b_a, rg_w_x, rg_b_x, rg_lambda) * jax.nn.gelu(b_gate, approximate=True)
    return jnp.concatenate([y_a, y_b], axis=-1) @ w_out


def stick_breaking_attention(h, w_qkv, w_o):
    bsz, s, _ = h.shape
    qkv = h @ w_qkv
    q, k, v = jnp.split(qkv, 3, axis=-1)
    to_heads = lambda t: t.reshape(bsz, s, SB_HEADS, SB_HEAD_DIM).transpose(0, 2, 1, 3)
    q, k, v = to_heads(q), to_heads(k), to_heads(v)
    n_blocks = s // Q_BLOCK
    q_blocks = q.reshape(bsz, SB_HEADS, n_blocks, Q_BLOCK, SB_HEAD_DIM).transpose(2, 0, 1, 3, 4)
    starts = jnp.arange(n_blocks, dtype=jnp.int32) * Q_BLOCK
    scale = 1.0 / math.sqrt(SB_HEAD_DIM)
    kf = k.astype(jnp.float32)
    vf = v.astype(jnp.float32)
    key_pos = jnp.arange(s, dtype=jnp.int32)[None, :]

    def block(args):
        q_blk, start = args
        z = jnp.einsum('bhqd,bhkd->bhqk', q_blk.astype(jnp.float32), kf) * scale
        q_pos = start + jnp.arange(Q_BLOCK, dtype=jnp.int32)[:, None]
        causal = key_pos < q_pos
        log_not = jnp.where(causal, jax.nn.log_sigmoid(-z), 0.0)
        suffix = lax.cumsum(log_not, axis=3, reverse=True) - log_not
        w = jnp.where(causal, jnp.exp(jax.nn.log_sigmoid(z) + suffix), 0.0)
        return jnp.einsum('bhqk,bhkd->bhqd', w, vf)

    out = lax.map(block, (q_blocks, starts))
    out = out.transpose(1, 0, 3, 2, 4).reshape(bsz, s, D_MODEL).astype(h.dtype)
    return out @ w_o


def sq_relu_mlp(h, w_up, w_down):
    u = jax.nn.relu(h @ w_up)
    return (u * u) @ w_down


def setup_inputs(seed: int = 0) -> dict:
    key = jax.random.key(seed)
    ks = jax.random.split(key, 20)
    nrm = lambda k, shape, fan_in: jax.random.normal(k, shape, jnp.float32) * (fan_in ** -0.5)
    x = jax.random.normal(ks[0], (BATCH, SEQ, D_MODEL), jnp.float32)
    norm_gains = 1.0 + 0.05 * jax.random.normal(ks[1], (DEPTH, 4, D_MODEL), jnp.float32)
    hyb_w_in = nrm(ks[2], (N_EVEN, D_MODEL, D_IN_EVEN), D_MODEL)
    hyb_conv_a = nrm(ks[3], (N_EVEN, CONV_A_WIDTH, D_A), CONV_A_WIDTH)
    hyb_conv_b = nrm(ks[4], (N_EVEN, CONV_B_WIDTH, D_B), CONV_B_WIDTH)
    hyb_conv_b_bias = 0.02 * jax.random.normal(ks[5], (N_EVEN, D_B), jnp.float32)
    hyb_rg_w_a = nrm(ks[6], (N_EVEN, B_HEADS, B_HEAD_DIM, B_HEAD_DIM), B_HEAD_DIM)
    hyb_rg_b_a = 0.02 * jax.random.normal(ks[7], (N_EVEN, D_B), jnp.float32)
    hyb_rg_w_x = nrm(ks[8], (N_EVEN, B_HEADS, B_HEAD_DIM, B_HEAD_DIM), B_HEAD_DIM)
    hyb_rg_b_x = 0.02 * jax.random.normal(ks[9], (N_EVEN, D_B), jnp.float32)
    u = jax.random.uniform(ks[10], (N_EVEN, D_B), jnp.float32, 0.9, 0.999)
    sig = u ** (1.0 / LRU_C)
    hyb_rg_lambda = jnp.log(sig) - jnp.log1p(-sig)
    hyb_w_out = nrm(ks[11], (N_EVEN, D_MIX, D_MODEL), D_MIX)
    sb_w_qkv = nrm(ks[12], (N_ODD, D_MODEL, 3 * D_MODEL), D_MODEL)
    sb_w_o = nrm(ks[13], (N_ODD, D_MODEL, D_MODEL), D_MODEL)
    mlp_w_up = nrm(ks[14], (DEPTH, D_MODEL, D_FF), D_MODEL)
    mlp_w_down = nrm(ks[15], (DEPTH, D_FF, D_MODEL), D_FF)
    return {"x": x, "norm_gains": norm_gains, "hyb_w_in": hyb_w_in, "hyb_conv_a": hyb_conv_a,
            "hyb_conv_b": hyb_conv_b, "hyb_conv_b_bias": hyb_conv_b_bias, "hyb_rg_w_a": hyb_rg_w_a,
            "hyb_rg_b_a": hyb_rg_b_a, "hyb_rg_w_x": hyb_rg_w_x, "hyb_rg_b_x": hyb_rg_b_x,
            "hyb_rg_lambda": hyb_rg_lambda, "hyb_w_out": hyb_w_out, "sb_w_qkv": sb_w_qkv,
            "sb_w_o": sb_w_o, "mlp_w_up": mlp_w_up, "mlp_w_down": mlp_w_down}


def reference(x, norm_gains, hyb_w_in, hyb_conv_a, hyb_conv_b, hyb_conv_b_bias, hyb_rg_w_a, hyb_rg_b_a,
              hyb_rg_w_x, hyb_rg_b_x, hyb_rg_lambda, hyb_w_out, sb_w_qkv, sb_w_o, mlp_w_up, mlp_w_down):
    for layer in range(DEPTH):
        g = norm_gains[layer]
        h = rms_norm(x, g[0])
        if layer % 2 == 0:
            e = layer // 2
            mix = conv_lru_mixer(h, hyb_w_in[e], hyb_conv_a[e], hyb_conv_b[e], hyb_conv_b_bias[e],
                                 hyb_rg_w_a[e], hyb_rg_b_a[e], hyb_rg_w_x[e], hyb_rg_b_x[e],
                                 hyb_rg_lambda[e], hyb_w_out[e])
        else:
            o = layer // 2
            mix = stick_breaking_attention(h, sb_w_qkv[o], sb_w_o[o])
        x = x + rms_norm(mix, g[1])
        h = rms_norm(x, g[2])
        x = x + rms_norm(sq_relu_mlp(h, mlp_w_up[layer], mlp_w_down[layer]), g[3])
    return x
```

```python
import functools
import math

import jax
import jax.numpy as jnp
from jax import lax
from jax.experimental import pallas as pl
from jax.experimental.pallas import tpu as pltpu

_NORM_EPS = 1e-6
_LRU_C = 8.0
_B_HEAD_DIM = 64
_SB_HEAD_DIM = 128
_CONV_A_WIDTH = 3
_CONV_B_WIDTH = 4

_VMEM_LIMIT_BYTES = 56 * 1024 * 1024

_BF16 = jnp.bfloat16
_F32 = jnp.float32


def _compiler_params(n_grid_axes):
    return pltpu.CompilerParams(
        dimension_semantics=("arbitrary",) * n_grid_axes,
        vmem_limit_bytes=_VMEM_LIMIT_BYTES,
    )


def _tile(dim, target):
    t = min(dim, target)
    while dim % t:
        t //= 2
    return t


def _rms_norm(x, g):
    ms = jnp.mean(x * x, axis=-1, keepdims=True)
    return x * lax.rsqrt(ms + _NORM_EPS) * g


def _sigmoid(x):
    return 1.0 / (1.0 + jnp.exp(-x))


def _prenorm_matmul_kernel(x_ref, g_ref, w_ref, o_ref, h_ref):
    @pl.when(pl.program_id(1) == 0)
    def _():
        h_ref[...] = _rms_norm(x_ref[...], g_ref[...]).astype(h_ref.dtype)

    o_ref[...] = jnp.dot(
        h_ref[...], w_ref[...], preferred_element_type=_F32
    ).astype(o_ref.dtype)


def _prenorm_matmul(x, g, w, *, out_dtype, tm=1024, tn=512):
    m, d = x.shape
    n = w.shape[1]
    tm, tn = _tile(m, tm), _tile(n, tn)
    return pl.pallas_call(
        _prenorm_matmul_kernel,
        grid=(m // tm, n // tn),
        in_specs=[
            pl.BlockSpec((tm, d), lambda i, j: (i, 0)),
            pl.BlockSpec((1, d), lambda i, j: (0, 0)),
            pl.BlockSpec((d, tn), lambda i, j: (0, j)),
        ],
        out_specs=pl.BlockSpec((tm, tn), lambda i, j: (i, j)),
        out_shape=jax.ShapeDtypeStruct((m, n), out_dtype),
        scratch_shapes=[pltpu.VMEM((tm, d), _BF16)],
        compiler_params=_compiler_params(2),
        name="prenorm_matmul",
    )(x, g, w)


def _matmul_postnorm_kernel(*refs, n_pairs):
    a_refs = refs[:n_pairs]
    w_refs = refs[n_pairs:2 * n_pairs]
    g_ref, res_ref, o_ref = refs[2 * n_pairs:]
    acc = None
    for a_ref, w_ref in zip(a_refs, w_refs):
        part = jnp.dot(a_ref[...].astype(_BF16), w_ref[...],
                       preferred_element_type=_F32)
        acc = part if acc is None else acc + part
    o_ref[...] = res_ref[...] + _rms_norm(acc, g_ref[...])


def _matmul_postnorm(a_list, w_list, g, res, *, tm=512):
    m, d = res.shape
    tm = _tile(m, tm)
    row_block = lambda i: (i, 0)
    whole = lambda i: (0, 0)
    return pl.pallas_call(
        functools.partial(_matmul_postnorm_kernel, n_pairs=len(a_list)),
        grid=(m // tm,),
        in_specs=(
            [pl.BlockSpec((tm, a.shape[1]), row_block) for a in a_list]
            + [pl.BlockSpec(w.shape, whole) for w in w_list]
            + [pl.BlockSpec((1, d), whole), pl.BlockSpec((tm, d), row_block)]
        ),
        out_specs=pl.BlockSpec((tm, d), row_block),
        out_shape=jax.ShapeDtypeStruct((m, d), _F32),
        compiler_params=_compiler_params(1),
        name="matmul_postnorm",
    )(*a_list, *w_list, g, res)


def _mlp_kernel(x_ref, g_in_ref, wu_ref, wd_ref, g_out_ref, o_ref, h_ref):
    f = pl.program_id(1)

    @pl.when(f == 0)
    def _():
        h_ref[...] = _rms_norm(x_ref[...], g_in_ref[...]).astype(h_ref.dtype)

    u = jnp.dot(h_ref[...], wu_ref[...], preferred_element_type=_F32)
    u = jnp.maximum(u, 0.0)
    part = jnp.dot((u * u).astype(_BF16), wd_ref[...],
                   preferred_element_type=_F32)

    @pl.when(f == 0)
    def _():
        o_ref[...] = part

    @pl.when(f > 0)
    def _():
        o_ref[...] += part

    @pl.when(f == pl.num_programs(1) - 1)
    def _():
        o_ref[...] = x_ref[...] + _rms_norm(o_ref[...], g_out_ref[...])


def _mlp(x, g_in, w_up, w_down, g_out, *, tm=512, tf=512):
    m, d = x.shape
    d_ff = w_up.shape[1]
    tm, tf = _tile(m, tm), _tile(d_ff, tf)
    return pl.pallas_call(
        _mlp_kernel,
        grid=(m // tm, d_ff // tf),
        in_specs=[
            pl.BlockSpec((tm, d), lambda i, f: (i, 0)),
            pl.BlockSpec((1, d), lambda i, f: (0, 0)),
            pl.BlockSpec((d, tf), lambda i, f: (0, f)),
            pl.BlockSpec((tf, d), lambda i, f: (f, 0)),
            pl.BlockSpec((1, d), lambda i, f: (0, 0)),
        ],
        out_specs=pl.BlockSpec((tm, d), lambda i, f: (i, 0)),
        out_shape=jax.ShapeDtypeStruct((m, d), _F32),
        scratch_shapes=[pltpu.VMEM((tm, d), _BF16)],
        compiler_params=_compiler_params(2),
        name="sq_relu_mlp",
    )(x, g_in, w_up, w_down, g_out)


def _mixer_kernel(bg_ref, cg_ref, ax_ref, gate_ref, bx_ref,
                  ca_ref, cb_ref, cbb_ref,
                  wa_ref, ba_ref, wx_ref, bxb_ref, lam_ref,
                  ya_ref, yb_ref,
                  u_ref, xs_ref, h_ref, *, tc):
    ts, bsz, cb = bg_ref.shape
    ka, kb = _CONV_A_WIDTH, _CONV_B_WIDTH

    @pl.when(pl.program_id(1) == 0)
    def _():
        u_ref[0:ka - 1] = jnp.zeros((ka - 1, bsz, cb), _F32)
        xs_ref[0:kb - 1] = jnp.zeros((kb - 1, bsz, cb), _F32)
        h_ref[...] = jnp.zeros((bsz, cb), _F32)

    c_log_sig = _LRU_C * jax.nn.log_sigmoid(lam_ref[...])

    def chunk(s, h):
        t0 = pl.multiple_of(s * tc, tc)
        cur = pl.ds(t0, tc)

        u_ref[pl.ds(t0 + ka - 1, tc)] = cg_ref[cur] * ax_ref[cur]
        conv = ca_ref[ka - 1] * u_ref[pl.ds(t0 + ka - 1, tc)]
        for k in range(ka - 1):
            conv = conv + ca_ref[k] * u_ref[pl.ds(t0 + k, tc)]
        ya_ref[cur] = bg_ref[cur] * conv

        xs_ref[pl.ds(t0 + kb - 1, tc)] = bx_ref[cur]
        xr = cb_ref[kb - 1] * xs_ref[pl.ds(t0 + kb - 1, tc)]
        for k in range(kb - 1):
            xr = xr + cb_ref[k] * xs_ref[pl.ds(t0 + k, tc)]
        xr = (xr + cbb_ref[...]).reshape(tc * bsz, cb)
        xr_bf = xr.astype(_BF16)
        r = _sigmoid(jnp.dot(xr_bf, wa_ref[...], preferred_element_type=_F32)
                     + ba_ref[...])
        gate_i = _sigmoid(jnp.dot(xr_bf, wx_ref[...], preferred_element_type=_F32)
                          + bxb_ref[...])
        log_a = c_log_sig * r
        a = jnp.exp(log_a).reshape(tc, bsz, cb)
        tanh_log_a = jnp.tanh(log_a)
        mult = jnp.sqrt(-2.0 * tanh_log_a / (1.0 - tanh_log_a))
        b = (mult * (gate_i * xr)).reshape(tc, bsz, cb)
        gelu_gate = jax.nn.gelu(gate_ref[cur], approximate=True)
        for t in range(tc):
            h = a[t] * h + b[t]
            yb_ref[t0 + t] = h * gelu_gate[t]
        return h

    h_ref[...] = lax.fori_loop(0, ts // tc, chunk, h_ref[...])
    u_ref[0:ka - 1] = u_ref[ts:ts + ka - 1]
    xs_ref[0:kb - 1] = xs_ref[ts:ts + kb - 1]


def _block_diag(w, per):
    h, d, _ = w.shape
    w4 = w.reshape(h // per, per, d, d)
    eye = jnp.eye(per, dtype=w.dtype)
    return jnp.einsum("cpij,pq->cpiqj", w4, eye).reshape(h // per, per * d, per * d)


def _mixer(proj, conv_a, conv_b, conv_b_bias, rg_w_a, rg_b_a, rg_w_x, rg_b_x,
           rg_lambda, *, ts=512, cb=128, tc=32):
    seq, bsz, _ = proj.shape
    d_a = conv_a.shape[1]
    d_b = conv_b.shape[1]
    assert d_a == d_b
    ts = _tile(seq, ts)
    nc = d_a // cb
    per = cb // _B_HEAD_DIM
    wa = _block_diag(rg_w_a, per).astype(_BF16)
    wx = _block_diag(rg_w_x, per).astype(_BF16)

    def seg_spec(seg):
        return pl.BlockSpec((ts, bsz, cb), lambda c, i: (i, 0, seg * nc + c))

    row_spec = pl.BlockSpec((1, cb), lambda c, i: (0, c))
    gate_w_spec = pl.BlockSpec((None, cb, cb), lambda c, i: (c, 0, 0))
    out_spec = pl.BlockSpec((ts, bsz, cb), lambda c, i: (i, 0, c))
    return pl.pallas_call(
        functools.partial(_mixer_kernel, tc=tc),
        grid=(nc, seq // ts),
        in_specs=[seg_spec(k) for k in range(5)] + [
            pl.BlockSpec((_CONV_A_WIDTH, 1, cb), lambda c, i: (0, 0, c)),
            pl.BlockSpec((_CONV_B_WIDTH, 1, cb), lambda c, i: (0, 0, c)),
            row_spec, gate_w_spec, row_spec, gate_w_spec, row_spec, row_spec,
        ],
        out_specs=[out_spec, out_spec],
        out_shape=[jax.ShapeDtypeStruct((seq, bsz, d_a), _F32),
                   jax.ShapeDtypeStruct((seq, bsz, d_b), _F32)],
        scratch_shapes=[
            pltpu.VMEM((ts + _CONV_A_WIDTH - 1, bsz, cb), _F32),
            pltpu.VMEM((ts + _CONV_B_WIDTH - 1, bsz, cb), _F32),
            pltpu.VMEM((bsz, cb), _F32),
        ],
        compiler_params=_compiler_params(2),
        name="conv_rglru_mixer",
    )(proj, proj, proj, proj, proj,
      conv_a.reshape(_CONV_A_WIDTH, 1, d_a), conv_b.reshape(_CONV_B_WIDTH, 1, d_b),
      conv_b_bias.reshape(1, d_b), wa, rg_b_a.reshape(1, d_b), wx,
      rg_b_x.reshape(1, d_b), rg_lambda.reshape(1, d_b))


def _attention_kernel(q_ref, k_ref, v_ref, o_ref, acc_ref, r_ref, *, tb):
    seq = q_ref.shape[0]
    row = lax.broadcasted_iota(jnp.int32, (tb, tb), 0)
    col = lax.broadcasted_iota(jnp.int32, (tb, tb), 1)
    causal = col < row
    later = jnp.where(row > col, 1.0, 0.0).astype(_BF16)

    def key_block(q, j0, on_diagonal):
        kb = k_ref[pl.ds(j0, tb), :]
        vb = v_ref[pl.ds(j0, tb), :]
        z = lax.dot_general(q, kb, (((1,), (1,)), ((), ())),
                            preferred_element_type=_F32)
        log_not = -(jnp.maximum(z, 0.0) + jnp.log1p(jnp.exp(-jnp.abs(z))))
        if on_diagonal:
            log_not = jnp.where(causal, log_not, 0.0)
        hi = log_not.astype(_BF16)
        lo = (log_not - hi.astype(_F32)).astype(_BF16)
        suffix = (jnp.dot(hi, later, preferred_element_type=_F32)
                  + jnp.dot(lo, later, preferred_element_type=_F32))
        w = jnp.exp(z + log_not + suffix + r_ref[...])
        if on_diagonal:
            w = jnp.where(causal, w, 0.0)
        acc_ref[...] += jnp.dot(w.astype(_BF16), vb, preferred_element_type=_F32)
        r_ref[...] += jnp.sum(log_not, axis=1, keepdims=True)

    def query_block(qi, carry):
        q0 = pl.multiple_of(qi * tb, tb)
        q = q_ref[pl.ds(q0, tb), :]
        acc_ref[...] = jnp.zeros_like(acc_ref)
        r_ref[...] = jnp.zeros_like(r_ref)
        key_block(q, q0, True)

        def earlier(n, c):
            key_block(q, pl.multiple_of((qi - 1 - n) * tb, tb), False)
            return c

        lax.fori_loop(0, qi, earlier, 0)
        o_ref[pl.ds(q0, tb), :] = acc_ref[...].astype(o_ref.dtype)
        return carry

    lax.fori_loop(0, seq // tb, query_block, 0)


def _attention(qkv, *, bsz, seq, n_heads, tb=256):
    dh = _SB_HEAD_DIM
    tb = _tile(seq, tb)
    blk = (seq, dh)
    return pl.pallas_call(
        functools.partial(_attention_kernel, tb=tb),
        grid=(bsz, n_heads),
        in_specs=[
            pl.BlockSpec(blk, lambda b, h: (b, h)),
            pl.BlockSpec(blk, lambda b, h: (b, n_heads + h)),
            pl.BlockSpec(blk, lambda b, h: (b, 2 * n_heads + h)),
        ],
        out_specs=pl.BlockSpec(blk, lambda b, h: (b, h)),
        out_shape=jax.ShapeDtypeStruct((bsz * seq, n_heads * dh), _BF16),
        scratch_shapes=[pltpu.VMEM((tb, dh), _F32), pltpu.VMEM((tb, 1), _F32)],
        compiler_params=_compiler_params(2),
        name="stick_breaking_attention",
    )(qkv, qkv, qkv)


def kernel(x, norm_gains, hyb_w_in, hyb_conv_a, hyb_conv_b, hyb_conv_b_bias,
           hyb_rg_w_a, hyb_rg_b_a, hyb_rg_w_x, hyb_rg_b_x, hyb_rg_lambda,
           hyb_w_out, sb_w_qkv, sb_w_o, mlp_w_up, mlp_w_down):
    bsz, seq, d = x.shape
    depth = norm_gains.shape[0]
    n_heads = d // _SB_HEAD_DIM
    for layer in range(depth):
        g = norm_gains[layer]
        w_up = mlp_w_up[layer].astype(_BF16)
        w_down = mlp_w_down[layer].astype(_BF16)
        if layer % 2 == 0:
            e = layer // 2
            d_a = hyb_conv_a.shape[2]
            xt = jnp.transpose(x, (1, 0, 2)).reshape(seq * bsz, d)
            proj = _prenorm_matmul(xt, g[0:1], hyb_w_in[e].astype(_BF16),
                                   out_dtype=_F32)
            y_a, y_b = _mixer(proj.reshape(seq, bsz, -1), hyb_conv_a[e],
                              hyb_conv_b[e], hyb_conv_b_bias[e], hyb_rg_w_a[e],
                              hyb_rg_b_a[e], hyb_rg_w_x[e], hyb_rg_b_x[e],
                              hyb_rg_lambda[e])
            w_out = hyb_w_out[e].astype(_BF16)
            xt = _matmul_postnorm(
                [y_a.reshape(seq * bsz, -1), y_b.reshape(seq * bsz, -1)],
                [w_out[:d_a], w_out[d_a:]], g[1:2], xt)
            xt = _mlp(xt, g[2:3], w_up, w_down, g[3:4])
            x = jnp.transpose(xt.reshape(seq, bsz, d), (1, 0, 2))
        else:
            o = layer // 2
            x2 = x.reshape(bsz * seq, d)
            col_scale = jnp.concatenate([
                jnp.full((d,), 1.0 / math.sqrt(_SB_HEAD_DIM), _F32),
                jnp.ones((2 * d,), _F32)])
            w_qkv = (sb_w_qkv[o] * col_scale).astype(_BF16)
            qkv = _prenorm_matmul(x2, g[0:1], w_qkv, out_dtype=_BF16)
            att = _attention(qkv, bsz=bsz, seq=seq, n_heads=n_heads)
            x2 = _matmul_postnorm([att], [sb_w_o[o].astype(_BF16)], g[1:2], x2)
            x2 = _mlp(x2, g[2:3], w_up, w_down, g[3:4])
            x = x2.reshape(bsz, seq, d)
    return x
```

```python
import functools
import math

import jax
import jax.numpy as jnp
from jax import lax
from jax.experimental import pallas as pl
from jax.experimental.pallas import tpu as pltpu

_NORM_EPS = 1e-6
_LRU_C = 8.0
_B_HEAD_DIM = 64
_SB_HEAD_DIM = 128
_CONV_A_WIDTH = 3
_CONV_B_WIDTH = 4
_F32_EXP_UNDERFLOW = -104.0

_VMEM_LIMIT_BYTES = 56 * 1024 * 1024

_BF16 = jnp.bfloat16
_F32 = jnp.float32


def _compiler_params(n_grid_axes):
    return pltpu.CompilerParams(
        dimension_semantics=("arbitrary",) * n_grid_axes,
        vmem_limit_bytes=_VMEM_LIMIT_BYTES,
    )


def _tile(dim, target):
    t = min(dim, target)
    while dim % t:
        t //= 2
    return t


def _rms_norm(x, g):
    ms = jnp.mean(x * x, axis=-1, keepdims=True)
    return x * lax.rsqrt(ms + _NORM_EPS) * g


def _sigmoid(x):
    return 1.0 / (1.0 + jnp.exp(-x))


def _prenorm_matmul_kernel(x_ref, g_ref, w_ref, o_ref, h_ref):
    @pl.when(pl.program_id(1) == 0)
    def _():
        h_ref[...] = _rms_norm(x_ref[...], g_ref[...]).astype(h_ref.dtype)

    o_ref[...] = jnp.dot(
        h_ref[...], w_ref[...], preferred_element_type=_F32
    ).astype(o_ref.dtype)


def _prenorm_matmul(x, g, w, *, out_dtype, tm=1024, tn=512):
    m, d = x.shape
    n = w.shape[1]
    tm, tn = _tile(m, tm), _tile(n, tn)
    return pl.pallas_call(
        _prenorm_matmul_kernel,
        grid=(m // tm, n // tn),
        in_specs=[
            pl.BlockSpec((tm, d), lambda i, j: (i, 0)),
            pl.BlockSpec((1, d), lambda i, j: (0, 0)),
            pl.BlockSpec((d, tn), lambda i, j: (0, j)),
        ],
        out_specs=pl.BlockSpec((tm, tn), lambda i, j: (i, j)),
        out_shape=jax.ShapeDtypeStruct((m, n), out_dtype),
        scratch_shapes=[pltpu.VMEM((tm, d), _BF16)],
        compiler_params=_compiler_params(2),
        name="prenorm_matmul",
    )(x, g, w)


def _matmul_postnorm_kernel(*refs, n_pairs):
    a_refs = refs[:n_pairs]
    w_refs = refs[n_pairs:2 * n_pairs]
    g_ref, res_ref, o_ref = refs[2 * n_pairs:]
    acc = None
    for a_ref, w_ref in zip(a_refs, w_refs):
        part = jnp.dot(a_ref[...].astype(_BF16), w_ref[...],
                       preferred_element_type=_F32)
        acc = part if acc is None else acc + part
    o_ref[...] = res_ref[...] + _rms_norm(acc, g_ref[...])


def _matmul_postnorm(a_list, w_list, g, res, *, tm=512):
    m, d = res.shape
    tm = _tile(m, tm)
    row_block = lambda i: (i, 0)
    whole = lambda i: (0, 0)
    return pl.pallas_call(
        functools.partial(_matmul_postnorm_kernel, n_pairs=len(a_list)),
        grid=(m // tm,),
        in_specs=(
            [pl.BlockSpec((tm, a.shape[1]), row_block) for a in a_list]
            + [pl.BlockSpec(w.shape, whole) for w in w_list]
            + [pl.BlockSpec((1, d), whole), pl.BlockSpec((tm, d), row_block)]
        ),
        out_specs=pl.BlockSpec((tm, d), row_block),
        out_shape=jax.ShapeDtypeStruct((m, d), _F32),
        compiler_params=_compiler_params(1),
        name="matmul_postnorm",
    )(*a_list, *w_list, g, res)


def _mlp_kernel(x_ref, g_in_ref, wu_ref, wd_ref, g_out_ref, o_ref, h_ref):
    f = pl.program_id(1)

    @pl.when(f == 0)
    def _():
        h_ref[...] = _rms_norm(x_ref[...], g_in_ref[...]).astype(h_ref.dtype)

    u = jnp.dot(h_ref[...], wu_ref[...], preferred_element_type=_F32)
    u = jnp.maximum(u, 0.0)
    part = jnp.dot((u * u).astype(_BF16), wd_ref[...],
                   preferred_element_type=_F32)

    @pl.when(f == 0)
    def _():
        o_ref[...] = part

    @pl.when(f > 0)
    def _():
        o_ref[...] += part

    @pl.when(f == pl.num_programs(1) - 1)
    def _():
        o_ref[...] = x_ref[...] + _rms_norm(o_ref[...], g_out_ref[...])


def _mlp(x, g_in, w_up, w_down, g_out, *, tm=512, tf=512):
    m, d = x.shape
    d_ff = w_up.shape[1]
    tm, tf = _tile(m, tm), _tile(d_ff, tf)
    return pl.pallas_call(
        _mlp_kernel,
        grid=(m // tm, d_ff // tf),
        in_specs=[
            pl.BlockSpec((tm, d), lambda i, f: (i, 0)),
            pl.BlockSpec((1, d), lambda i, f: (0, 0)),
            pl.BlockSpec((d, tf), lambda i, f: (0, f)),
            pl.BlockSpec((tf, d), lambda i, f: (f, 0)),
            pl.BlockSpec((1, d), lambda i, f: (0, 0)),
        ],
        out_specs=pl.BlockSpec((tm, d), lambda i, f: (i, 0)),
        out_shape=jax.ShapeDtypeStruct((m, d), _F32),
        scratch_shapes=[pltpu.VMEM((tm, d), _BF16)],
        compiler_params=_compiler_params(2),
        name="sq_relu_mlp",
    )(x, g_in, w_up, w_down, g_out)


def _mixer_kernel(bg_ref, cg_ref, ax_ref, gate_ref, bx_ref,
                  ca_ref, cb_ref, cbb_ref,
                  wa_ref, ba_ref, wx_ref, bxb_ref, lam_ref,
                  ya_ref, yb_ref,
                  u_ref, xs_ref, h_ref, *, tc):
    ts, bsz, cb = bg_ref.shape
    ka, kb = _CONV_A_WIDTH, _CONV_B_WIDTH

    @pl.when(pl.program_id(1) == 0)
    def _():
        u_ref[0:ka - 1] = jnp.zeros((ka - 1, bsz, cb), _F32)
        xs_ref[0:kb - 1] = jnp.zeros((kb - 1, bsz, cb), _F32)
        h_ref[...] = jnp.zeros((bsz, cb), _F32)

    c_log_sig = _LRU_C * jax.nn.log_sigmoid(lam_ref[...])

    def chunk(s, h):
        t0 = pl.multiple_of(s * tc, tc)
        cur = pl.ds(t0, tc)

        u_ref[pl.ds(t0 + ka - 1, tc)] = cg_ref[cur] * ax_ref[cur]
        conv = ca_ref[ka - 1] * u_ref[pl.ds(t0 + ka - 1, tc)]
        for k in range(ka - 1):
            conv = conv + ca_ref[k] * u_ref[pl.ds(t0 + k, tc)]
        ya_ref[cur] = bg_ref[cur] * conv

        xs_ref[pl.ds(t0 + kb - 1, tc)] = bx_ref[cur]
        xr = cb_ref[kb - 1] * xs_ref[pl.ds(t0 + kb - 1, tc)]
        for k in range(kb - 1):
            xr = xr + cb_ref[k] * xs_ref[pl.ds(t0 + k, tc)]
        xr = (xr + cbb_ref[...]).reshape(tc * bsz, cb)
        xr_bf = xr.astype(_BF16)
        r = _sigmoid(jnp.dot(xr_bf, wa_ref[...], preferred_element_type=_F32)
                     + ba_ref[...])
        gate_i = _sigmoid(jnp.dot(xr_bf, wx_ref[...], preferred_element_type=_F32)
                          + bxb_ref[...])
        log_a = c_log_sig * r
        a = jnp.exp(log_a).reshape(tc, bsz, cb)
        tanh_log_a = jnp.tanh(log_a)
        mult = jnp.sqrt(-2.0 * tanh_log_a / (1.0 - tanh_log_a))
        b = (mult * (gate_i * xr)).reshape(tc, bsz, cb)
        gelu_gate = jax.nn.gelu(gate_ref[cur], approximate=True)
        for t in range(tc):
            h = a[t] * h + b[t]
            yb_ref[t0 + t] = h * gelu_gate[t]
        return h

    h_ref[...] = lax.fori_loop(0, ts // tc, chunk, h_ref[...])
    u_ref[0:ka - 1] = u_ref[ts:ts + ka - 1]
    xs_ref[0:kb - 1] = xs_ref[ts:ts + kb - 1]


def _block_diag(w, per):
    h, d, _ = w.shape
    w4 = w.reshape(h // per, per, d, d)
    eye = jnp.eye(per, dtype=w.dtype)
    return jnp.einsum("cpij,pq->cpiqj", w4, eye).reshape(h // per, per * d, per * d)


def _mixer(proj, conv_a, conv_b, conv_b_bias, rg_w_a, rg_b_a, rg_w_x, rg_b_x,
           rg_lambda, *, ts=512, cb=128, tc=32):
    seq, bsz, _ = proj.shape
    d_a = conv_a.shape[1]
    d_b = conv_b.shape[1]
    assert d_a == d_b
    ts = _tile(seq, ts)
    nc = d_a // cb
    per = cb // _B_HEAD_DIM
    wa = _block_diag(rg_w_a, per).astype(_BF16)
    wx = _block_diag(rg_w_x, per).astype(_BF16)

    def seg_spec(seg):
        return pl.BlockSpec((ts, bsz, cb), lambda c, i: (i, 0, seg * nc + c))

    row_spec = pl.BlockSpec((1, cb), lambda c, i: (0, c))
    gate_w_spec = pl.BlockSpec((None, cb, cb), lambda c, i: (c, 0, 0))
    out_spec = pl.BlockSpec((ts, bsz, cb), lambda c, i: (i, 0, c))
    return pl.pallas_call(
        functools.partial(_mixer_kernel, tc=tc),
        grid=(nc, seq // ts),
        in_specs=[seg_spec(k) for k in range(5)] + [
            pl.BlockSpec((_CONV_A_WIDTH, 1, cb), lambda c, i: (0, 0, c)),
            pl.BlockSpec((_CONV_B_WIDTH, 1, cb), lambda c, i: (0, 0, c)),
            row_spec, gate_w_spec, row_spec, gate_w_spec, row_spec, row_spec,
        ],
        out_specs=[out_spec, out_spec],
        out_shape=[jax.ShapeDtypeStruct((seq, bsz, d_a), _F32),
                   jax.ShapeDtypeStruct((seq, bsz, d_b), _F32)],
        scratch_shapes=[
            pltpu.VMEM((ts + _CONV_A_WIDTH - 1, bsz, cb), _F32),
            pltpu.VMEM((ts + _CONV_B_WIDTH - 1, bsz, cb), _F32),
            pltpu.VMEM((bsz, cb), _F32),
        ],
        compiler_params=_compiler_params(2),
        name="conv_rglru_mixer",
    )(proj, proj, proj, proj, proj,
      conv_a.reshape(_CONV_A_WIDTH, 1, d_a), conv_b.reshape(_CONV_B_WIDTH, 1, d_b),
      conv_b_bias.reshape(1, d_b), wa, rg_b_a.reshape(1, d_b), wx,
      rg_b_x.reshape(1, d_b), rg_lambda.reshape(1, d_b))


def _attention_kernel(q_ref, k_ref, v_ref, o_ref, acc_ref, r_ref, *, tb):
    seq = q_ref.shape[0]
    row = lax.broadcasted_iota(jnp.int32, (tb, tb), 0)
    col = lax.broadcasted_iota(jnp.int32, (tb, tb), 1)
    causal = col < row
    later = jnp.where(row > col, 1.0, 0.0).astype(_BF16)

    def key_block(q, j0, on_diagonal):
        kb = k_ref[pl.ds(j0, tb), :]
        vb = v_ref[pl.ds(j0, tb), :]
        z = lax.dot_general(q, kb, (((1,), (1,)), ((), ())),
                            preferred_element_type=_F32)
        log_not = -(jnp.maximum(z, 0.0) + jnp.log1p(jnp.exp(-jnp.abs(z))))
        if on_diagonal:
            log_not = jnp.where(causal, log_not, 0.0)
        hi = log_not.astype(_BF16)
        lo = (log_not - hi.astype(_F32)).astype(_BF16)
        suffix = (jnp.dot(hi, later, preferred_element_type=_F32)
                  + jnp.dot(lo, later, preferred_element_type=_F32))
        w = jnp.exp(z + log_not + suffix + r_ref[...])
        if on_diagonal:
            w = jnp.where(causal, w, 0.0)
        acc_ref[...] += jnp.dot(w.astype(_BF16), vb, preferred_element_type=_F32)
        r_ref[...] += jnp.sum(log_not, axis=1, keepdims=True)

    def query_block(qi, carry):
        q0 = pl.multiple_of(qi * tb, tb)
        q = q_ref[pl.ds(q0, tb), :]
        acc_ref[...] = jnp.zeros_like(acc_ref)
        r_ref[...] = jnp.zeros_like(r_ref)
        key_block(q, q0, True)

        def more(c):
            n, r_max = c
            return jnp.logical_and(n < qi, r_max > _F32_EXP_UNDERFLOW)

        def earlier(c):
            n, _ = c
            key_block(q, pl.multiple_of((qi - 1 - n) * tb, tb), False)
            return n + 1, jnp.max(r_ref[...])

        lax.while_loop(more, earlier, (0, jnp.max(r_ref[...])))
        o_ref[pl.ds(q0, tb), :] = acc_ref[...].astype(o_ref.dtype)
        return carry

    lax.fori_loop(0, seq // tb, query_block, 0)


def _attention(qkv, *, bsz, seq, n_heads, tb=256):
    dh = _SB_HEAD_DIM
    tb = _tile(seq, tb)
    blk = (seq, dh)
    return pl.pallas_call(
        functools.partial(_attention_kernel, tb=tb),
        grid=(bsz, n_heads),
        in_specs=[
            pl.BlockSpec(blk, lambda b, h: (b, h)),
            pl.BlockSpec(blk, lambda b, h: (b, n_heads + h)),
            pl.BlockSpec(blk, lambda b, h: (b, 2 * n_heads + h)),
        ],
        out_specs=pl.BlockSpec(blk, lambda b, h: (b, h)),
        out_shape=jax.ShapeDtypeStruct((bsz * seq, n_heads * dh), _BF16),
        scratch_shapes=[pltpu.VMEM((tb, dh), _F32), pltpu.VMEM((tb, 1), _F32)],
        compiler_params=_compiler_params(2),
        name="stick_breaking_attention",
    )(qkv, qkv, qkv)


def kernel(x, norm_gains, hyb_w_in, hyb_conv_a, hyb_conv_b, hyb_conv_b_bias,
           hyb_rg_w_a, hyb_rg_b_a, hyb_rg_w_x, hyb_rg_b_x, hyb_rg_lambda,
           hyb_w_out, sb_w_qkv, sb_w_o, mlp_w_up, mlp_w_down):
    bsz, seq, d = x.shape
    depth = norm_gains.shape[0]
    n_heads = d // _SB_HEAD_DIM
    for layer in range(depth):
        g = norm_gains[layer]
        w_up = mlp_w_up[layer].astype(_BF16)
        w_down = mlp_w_down[layer].astype(_BF16)
        if layer % 2 == 0:
            e = layer // 2
            d_a = hyb_conv_a.shape[2]
            xt = jnp.transpose(x, (1, 0, 2)).reshape(seq * bsz, d)
            proj = _prenorm_matmul(xt, g[0:1], hyb_w_in[e].astype(_BF16),
                                   out_dtype=_F32)
            y_a, y_b = _mixer(proj.reshape(seq, bsz, -1), hyb_conv_a[e],
                              hyb_conv_b[e], hyb_conv_b_bias[e], hyb_rg_w_a[e],
                              hyb_rg_b_a[e], hyb_rg_w_x[e], hyb_rg_b_x[e],
                              hyb_rg_lambda[e])
            w_out = hyb_w_out[e].astype(_BF16)
            xt = _matmul_postnorm(
                [y_a.reshape(seq * bsz, -1), y_b.reshape(seq * bsz, -1)],
                [w_out[:d_a], w_out[d_a:]], g[1:2], xt)
            xt = _mlp(xt, g[2:3], w_up, w_down, g[3:4])
            x = jnp.transpose(xt.reshape(seq, bsz, d), (1, 0, 2))
        else:
            o = layer // 2
            x2 = x.reshape(bsz * seq, d)
            col_scale = jnp.concatenate([
                jnp.full((d,), 1.0 / math.sqrt(_SB_HEAD_DIM), _F32),
                jnp.ones((2 * d,), _F32)])
            w_qkv = (sb_w_qkv[o] * col_scale).astype(_BF16)
            qkv = _prenorm_matmul(x2, g[0:1], w_qkv, out_dtype=_BF16)
            att = _attention(qkv, bsz=bsz, seq=seq, n_heads=n_heads)
            x2 = _matmul_postnorm([att], [sb_w_o[o].astype(_BF16)], g[1:2], x2)
            x2 = _mlp(x2, g[2:3], w_up, w_down, g[3:4])
            x = x2.reshape(bsz, seq, d)
    return x
```

```python
import functools
import math

import jax
import jax.numpy as jnp
from jax import lax
from jax.experimental import pallas as pl
from jax.experimental.pallas import tpu as pltpu

_NORM_EPS = 1e-6
_LRU_C = 8.0
_B_HEAD_DIM = 64
_SB_HEAD_DIM = 128
_CONV_A_WIDTH = 3
_CONV_B_WIDTH = 4
_F32_EXP_UNDERFLOW = -104.0

_VMEM_LIMIT_BYTES = 56 * 1024 * 1024

_BF16 = jnp.bfloat16
_F32 = jnp.float32


def _compiler_params(n_grid_axes):
    return pltpu.CompilerParams(
        dimension_semantics=("arbitrary",) * n_grid_axes,
        vmem_limit_bytes=_VMEM_LIMIT_BYTES,
    )


def _tile(dim, target):
    t = min(dim, target)
    while dim % t:
        t //= 2
    return t


def _rms_norm(x, g):
    ms = jnp.mean(x * x, axis=-1, keepdims=True)
    return x * lax.rsqrt(ms + _NORM_EPS) * g


def _sigmoid(x):
    return 1.0 / (1.0 + jnp.exp(-x))


def _prenorm_matmul_kernel(x_ref, g_ref, w_ref, o_ref, h_ref):
    @pl.when(pl.program_id(1) == 0)
    def _():
        h_ref[...] = _rms_norm(x_ref[...], g_ref[...]).astype(h_ref.dtype)

    o_ref[...] = jnp.dot(
        h_ref[...], w_ref[...], preferred_element_type=_F32
    ).astype(o_ref.dtype)


def _prenorm_matmul(x, g, w, *, out_dtype, tm=1024, tn=1024):
    m, d = x.shape
    n = w.shape[1]
    tm, tn = _tile(m, tm), _tile(n, tn)
    return pl.pallas_call(
        _prenorm_matmul_kernel,
        grid=(m // tm, n // tn),
        in_specs=[
            pl.BlockSpec((tm, d), lambda i, j: (i, 0)),
            pl.BlockSpec((1, d), lambda i, j: (0, 0)),
            pl.BlockSpec((d, tn), lambda i, j: (0, j)),
        ],
        out_specs=pl.BlockSpec((tm, tn), lambda i, j: (i, j)),
        out_shape=jax.ShapeDtypeStruct((m, n), out_dtype),
        scratch_shapes=[pltpu.VMEM((tm, d), _BF16)],
        compiler_params=_compiler_params(2),
        name="prenorm_matmul",
    )(x, g, w)


def _matmul_postnorm_kernel(*refs, n_pairs):
    a_refs = refs[:n_pairs]
    w_refs = refs[n_pairs:2 * n_pairs]
    g_ref, res_ref, o_ref = refs[2 * n_pairs:]
    acc = None
    for a_ref, w_ref in zip(a_refs, w_refs):
        part = jnp.dot(a_ref[...].astype(_BF16), w_ref[...],
                       preferred_element_type=_F32)
        acc = part if acc is None else acc + part
    o_ref[...] = res_ref[...] + _rms_norm(acc, g_ref[...])


def _matmul_postnorm(a_list, w_list, g, res, *, tm=512):
    m, d = res.shape
    tm = _tile(m, tm)
    row_block = lambda i: (i, 0)
    whole = lambda i: (0, 0)
    return pl.pallas_call(
        functools.partial(_matmul_postnorm_kernel, n_pairs=len(a_list)),
        grid=(m // tm,),
        in_specs=(
            [pl.BlockSpec((tm, a.shape[1]), row_block) for a in a_list]
            + [pl.BlockSpec(w.shape, whole) for w in w_list]
            + [pl.BlockSpec((1, d), whole), pl.BlockSpec((tm, d), row_block)]
        ),
        out_specs=pl.BlockSpec((tm, d), row_block),
        out_shape=jax.ShapeDtypeStruct((m, d), _F32),
        compiler_params=_compiler_params(1),
        name="matmul_postnorm",
    )(*a_list, *w_list, g, res)


def _mlp_kernel(x_ref, g_in_ref, wu_ref, wd_ref, g_out_ref, o_ref, h_ref):
    f = pl.program_id(1)

    @pl.when(f == 0)
    def _():
        h_ref[...] = _rms_norm(x_ref[...], g_in_ref[...]).astype(h_ref.dtype)
        o_ref[...] = jnp.zeros_like(o_ref)

    u = jnp.dot(h_ref[...], wu_ref[...], preferred_element_type=_F32)
    u = jnp.maximum(u, 0.0)
    o_ref[...] += jnp.dot((u * u).astype(_BF16), wd_ref[...],
                          preferred_element_type=_F32)

    @pl.when(f == pl.num_programs(1) - 1)
    def _():
        o_ref[...] = x_ref[...] + _rms_norm(o_ref[...], g_out_ref[...])


def _mlp(x, g_in, w_up, w_down, g_out, *, tm=512, tf=1024):
    m, d = x.shape
    d_ff = w_up.shape[1]
    tm, tf = _tile(m, tm), _tile(d_ff, tf)
    return pl.pallas_call(
        _mlp_kernel,
        grid=(m // tm, d_ff // tf),
        in_specs=[
            pl.BlockSpec((tm, d), lambda i, f: (i, 0)),
            pl.BlockSpec((1, d), lambda i, f: (0, 0)),
            pl.BlockSpec((d, tf), lambda i, f: (0, f)),
            pl.BlockSpec((tf, d), lambda i, f: (f, 0)),
            pl.BlockSpec((1, d), lambda i, f: (0, 0)),
        ],
        out_specs=pl.BlockSpec((tm, d), lambda i, f: (i, 0)),
        out_shape=jax.ShapeDtypeStruct((m, d), _F32),
        scratch_shapes=[pltpu.VMEM((tm, d), _BF16)],
        compiler_params=_compiler_params(2),
        name="sq_relu_mlp",
    )(x, g_in, w_up, w_down, g_out)


def _mixer_kernel(bg_ref, cg_ref, ax_ref, gate_ref, bx_ref,
                  ca_ref, cb_ref, cbb_ref,
                  wa_ref, ba_ref, wx_ref, bxb_ref, lam_ref,
                  ya_ref, yb_ref,
                  u_ref, xs_ref, h_ref, *, tc):
    ts, bsz, cb = bg_ref.shape
    ka, kb = _CONV_A_WIDTH, _CONV_B_WIDTH

    @pl.when(pl.program_id(1) == 0)
    def _():
        u_ref[0:ka - 1] = jnp.zeros((ka - 1, bsz, cb), _F32)
        xs_ref[0:kb - 1] = jnp.zeros((kb - 1, bsz, cb), _F32)
        h_ref[...] = jnp.zeros((bsz, cb), _F32)

    c_log_sig = _LRU_C * jax.nn.log_sigmoid(lam_ref[...])

    def chunk(s, h):
        t0 = pl.multiple_of(s * tc, tc)
        cur = pl.ds(t0, tc)

        u_ref[pl.ds(t0 + ka - 1, tc)] = cg_ref[cur] * ax_ref[cur]
        conv = ca_ref[ka - 1] * u_ref[pl.ds(t0 + ka - 1, tc)]
        for k in range(ka - 1):
            conv = conv + ca_ref[k] * u_ref[pl.ds(t0 + k, tc)]
        ya_ref[cur] = bg_ref[cur] * conv

        xs_ref[pl.ds(t0 + kb - 1, tc)] = bx_ref[cur]
        xr = cb_ref[kb - 1] * xs_ref[pl.ds(t0 + kb - 1, tc)]
        for k in range(kb - 1):
            xr = xr + cb_ref[k] * xs_ref[pl.ds(t0 + k, tc)]
        xr = (xr + cbb_ref[...]).reshape(tc * bsz, cb)
        xr_bf = xr.astype(_BF16)
        r = _sigmoid(jnp.dot(xr_bf, wa_ref[...], preferred_element_type=_F32)
                     + ba_ref[...])
        gate_i = _sigmoid(jnp.dot(xr_bf, wx_ref[...], preferred_element_type=_F32)
                          + bxb_ref[...])
        log_a = c_log_sig * r
        a = jnp.exp(log_a).reshape(tc, bsz, cb)
        tanh_log_a = jnp.tanh(log_a)
        mult = jnp.sqrt(-2.0 * tanh_log_a / (1.0 - tanh_log_a))
        b = (mult * (gate_i * xr)).reshape(tc, bsz, cb)
        gelu_gate = jax.nn.gelu(gate_ref[cur], approximate=True)
        for t in range(tc):
            h = a[t] * h + b[t]
            yb_ref[t0 + t] = h * gelu_gate[t]
        return h

    h_ref[...] = lax.fori_loop(0, ts // tc, chunk, h_ref[...])
    u_ref[0:ka - 1] = u_ref[ts:ts + ka - 1]
    xs_ref[0:kb - 1] = xs_ref[ts:ts + kb - 1]


def _block_diag(w, per):
    h, d, _ = w.shape
    w4 = w.reshape(h // per, per, d, d)
    eye = jnp.eye(per, dtype=w.dtype)
    return jnp.einsum("cpij,pq->cpiqj", w4, eye).reshape(h // per, per * d, per * d)


def _mixer(proj, conv_a, conv_b, conv_b_bias, rg_w_a, rg_b_a, rg_w_x, rg_b_x,
           rg_lambda, *, ts=512, cb=128, tc=32):
    seq, bsz, _ = proj.shape
    d_a = conv_a.shape[1]
    d_b = conv_b.shape[1]
    assert d_a == d_b
    ts = _tile(seq, ts)
    nc = d_a // cb
    per = cb // _B_HEAD_DIM
    wa = _block_diag(rg_w_a, per).astype(_BF16)
    wx = _block_diag(rg_w_x, per).astype(_BF16)

    def seg_spec(seg):
        return pl.BlockSpec((ts, bsz, cb), lambda c, i: (i, 0, seg * nc + c))

    row_spec = pl.BlockSpec((1, cb), lambda c, i: (0, c))
    gate_w_spec = pl.BlockSpec((None, cb, cb), lambda c, i: (c, 0, 0))
    out_spec = pl.BlockSpec((ts, bsz, cb), lambda c, i: (i, 0, c))
    return pl.pallas_call(
        functools.partial(_mixer_kernel, tc=tc),
        grid=(nc, seq // ts),
        in_specs=[seg_spec(k) for k in range(5)] + [
            pl.BlockSpec((_CONV_A_WIDTH, 1, cb), lambda c, i: (0, 0, c)),
            pl.BlockSpec((_CONV_B_WIDTH, 1, cb), lambda c, i: (0, 0, c)),
            row_spec, gate_w_spec, row_spec, gate_w_spec, row_spec, row_spec,
        ],
        out_specs=[out_spec, out_spec],
        out_shape=[jax.ShapeDtypeStruct((seq, bsz, d_a), _F32),
                   jax.ShapeDtypeStruct((seq, bsz, d_b), _F32)],
        scratch_shapes=[
            pltpu.VMEM((ts + _CONV_A_WIDTH - 1, bsz, cb), _F32),
            pltpu.VMEM((ts + _CONV_B_WIDTH - 1, bsz, cb), _F32),
            pltpu.VMEM((bsz, cb), _F32),
        ],
        compiler_params=_compiler_params(2),
        name="conv_rglru_mixer",
    )(proj, proj, proj, proj, proj,
      conv_a.reshape(_CONV_A_WIDTH, 1, d_a), conv_b.reshape(_CONV_B_WIDTH, 1, d_b),
      conv_b_bias.reshape(1, d_b), wa, rg_b_a.reshape(1, d_b), wx,
      rg_b_x.reshape(1, d_b), rg_lambda.reshape(1, d_b))


def _attention_kernel(q_ref, k_ref, v_ref, o_ref, acc_ref, r_ref, *, tb):
    seq = q_ref.shape[0]
    dh = _SB_HEAD_DIM
    heads = range(q_ref.shape[1] // dh)
    row = lax.broadcasted_iota(jnp.int32, (tb, tb), 0)
    col = lax.broadcasted_iota(jnp.int32, (tb, tb), 1)
    causal = col < row
    later = jnp.where(row > col, 1.0, 0.0).astype(_BF16)

    def scores(h, q0, j0, nb):
        lanes = slice(h * dh, (h + 1) * dh)
        return lax.dot_general(
            q_ref[pl.ds(q0, tb), lanes], k_ref[pl.ds(j0, nb * tb), lanes],
            (((1,), (1,)), ((), ())), preferred_element_type=_F32)

    def log_not_beta(z, valid):
        neg_z = -z
        out = (jnp.minimum(neg_z, 0.0)
               - jnp.log(1.0 + jnp.exp(jnp.minimum(z, neg_z))))
        return out if valid is None else jnp.where(valid, out, 0.0)

    def block_suffix(log_not, nb):
        stacked = jnp.concatenate(
            [log_not[:, i * tb:(i + 1) * tb] for i in range(nb)], axis=0)
        hi = stacked.astype(_BF16)
        lo = (stacked - hi.astype(_F32)).astype(_BF16)
        return (jnp.dot(hi, later, preferred_element_type=_F32)
                + jnp.dot(lo, later, preferred_element_type=_F32))

    def weights(h, z, log_not, suffix, nb, valid):
        nearer = r_ref[h]
        log_w = [None] * nb
        for i in reversed(range(nb)):
            cols = slice(i * tb, (i + 1) * tb)
            log_w[i] = z[:, cols] + log_not[:, cols] + suffix[cols] + nearer
            nearer = nearer + jnp.sum(log_not[:, cols], axis=1, keepdims=True)
        r_ref[h] = nearer
        w = jnp.exp(jnp.concatenate(log_w, axis=1))
        return w if valid is None else jnp.where(valid, w, 0.0)

    def accumulate(h, w, j0, nb):
        lanes = slice(h * dh, (h + 1) * dh)
        acc_ref[:, lanes] += jnp.dot(
            w.astype(_BF16), v_ref[pl.ds(j0, nb * tb), lanes],
            preferred_element_type=_F32)

    def key_window(q0, j0, nb, ends_on_diagonal):
        valid = None
        if ends_on_diagonal:
            valid = causal if nb == 1 else jnp.concatenate(
                [jnp.ones((tb, (nb - 1) * tb), jnp.bool_), causal], axis=1)
        z = [scores(h, q0, j0, nb) for h in heads]
        log_not = [log_not_beta(z[h], valid) for h in heads]
        suffix = [block_suffix(log_not[h], nb) for h in heads]
        w = [weights(h, z[h], log_not[h], suffix[h], nb, valid) for h in heads]
        for h in heads:
            accumulate(h, w[h], j0, nb)

    def start_block():
        acc_ref[...] = jnp.zeros_like(acc_ref)
        r_ref[...] = jnp.zeros_like(r_ref)

    def finish_block(q0):
        o_ref[pl.ds(q0, tb), :] = acc_ref[...].astype(o_ref.dtype)

    def query_block(qi, carry):
        q0 = pl.multiple_of(qi * tb, tb)
        start_block()
        key_window(q0, q0 - tb, 2, True)

        def more(c):
            n, r_max = c
            return jnp.logical_and(n < qi - 1, r_max > _F32_EXP_UNDERFLOW)

        def earlier(c):
            n, _ = c
            key_window(q0, pl.multiple_of((qi - 2 - n) * tb, tb), 1, False)
            return n + 1, jnp.max(r_ref[...])

        lax.while_loop(more, earlier, (0, jnp.max(r_ref[...])))
        finish_block(q0)
        return carry

    start_block()
    key_window(0, 0, 1, True)
    finish_block(0)
    lax.fori_loop(1, seq // tb, query_block, 0)


def _attention(qkv, *, bsz, seq, n_heads, tb=256, heads_per_step=4):
    dh = _SB_HEAD_DIM
    tb = _tile(seq, tb)
    hp = _tile(n_heads, heads_per_step)
    ng = n_heads // hp
    blk = (seq, hp * dh)
    return pl.pallas_call(
        functools.partial(_attention_kernel, tb=tb),
        grid=(bsz, ng),
        in_specs=[
            pl.BlockSpec(blk, lambda b, g: (b, g)),
            pl.BlockSpec(blk, lambda b, g: (b, ng + g)),
            pl.BlockSpec(blk, lambda b, g: (b, 2 * ng + g)),
        ],
        out_specs=pl.BlockSpec(blk, lambda b, g: (b, g)),
        out_shape=jax.ShapeDtypeStruct((bsz * seq, n_heads * dh), _BF16),
        scratch_shapes=[pltpu.VMEM((tb, hp * dh), _F32),
                        pltpu.VMEM((hp, tb, 1), _F32)],
        compiler_params=_compiler_params(2),
        name="stick_breaking_attention",
    )(qkv, qkv, qkv)


def kernel(x, norm_gains, hyb_w_in, hyb_conv_a, hyb_conv_b, hyb_conv_b_bias,
           hyb_rg_w_a, hyb_rg_b_a, hyb_rg_w_x, hyb_rg_b_x, hyb_rg_lambda,
           hyb_w_out, sb_w_qkv, sb_w_o, mlp_w_up, mlp_w_down):
    bsz, seq, d = x.shape
    depth = norm_gains.shape[0]
    n_heads = d // _SB_HEAD_DIM
    for layer in range(depth):
        g = norm_gains[layer]
        w_up = mlp_w_up[layer].astype(_BF16)
        w_down = mlp_w_down[layer].astype(_BF16)
        if layer % 2 == 0:
            e = layer // 2
            d_a = hyb_conv_a.shape[2]
            xt = jnp.transpose(x, (1, 0, 2)).reshape(seq * bsz, d)
            proj = _prenorm_matmul(xt, g[0:1], hyb_w_in[e].astype(_BF16),
                                   out_dtype=_F32)
            y_a, y_b = _mixer(proj.reshape(seq, bsz, -1), hyb_conv_a[e],
                              hyb_conv_b[e], hyb_conv_b_bias[e], hyb_rg_w_a[e],
                              hyb_rg_b_a[e], hyb_rg_w_x[e], hyb_rg_b_x[e],
                              hyb_rg_lambda[e])
            w_out = hyb_w_out[e].astype(_BF16)
            xt = _matmul_postnorm(
                [y_a.reshape(seq * bsz, -1), y_b.reshape(seq * bsz, -1)],
                [w_out[:d_a], w_out[d_a:]], g[1:2], xt)
            xt = _mlp(xt, g[2:3], w_up, w_down, g[3:4])
            x = jnp.transpose(xt.reshape(seq, bsz, d), (1, 0, 2))
        else:
            o = layer // 2
            x2 = x.reshape(bsz * seq, d)
            col_scale = jnp.concatenate([
                jnp.full((d,), 1.0 / math.sqrt(_SB_HEAD_DIM), _F32),
                jnp.ones((2 * d,), _F32)])
            w_qkv = (sb_w_qkv[o] * col_scale).astype(_BF16)
            qkv = _prenorm_matmul(x2, g[0:1], w_qkv, out_dtype=_BF16)
            att = _attention(qkv, bsz=bsz, seq=seq, n_heads=n_heads)
            x2 = _matmul_postnorm([att], [sb_w_o[o].astype(_BF16)], g[1:2], x2)
            x2 = _mlp(x2, g[2:3], w_up, w_down, g[3:4])
            x = x2.reshape(bsz, seq, d)
    return x
```

```python
import functools
import math

import jax
import jax.numpy as jnp
from jax import lax
from jax.experimental import pallas as pl
from jax.experimental.pallas import tpu as pltpu

_NORM_EPS = 1e-6
_LRU_C = 8.0
_B_HEAD_DIM = 64
_SB_HEAD_DIM = 128
_CONV_A_WIDTH = 3
_CONV_B_WIDTH = 4
_PERM_STEPS = 32
_F32_EXP_UNDERFLOW = -104.0

_VMEM_LIMIT_BYTES = 56 * 1024 * 1024

_BF16 = jnp.bfloat16
_F32 = jnp.float32


def _compiler_params(n_grid_axes):
    return pltpu.CompilerParams(
        dimension_semantics=("arbitrary",) * n_grid_axes,
        vmem_limit_bytes=_VMEM_LIMIT_BYTES,
    )


def _tile(dim, target):
    t = min(dim, target)
    while dim % t:
        t //= 2
    return t


def _rms_norm(x, g):
    ms = jnp.mean(x * x, axis=-1, keepdims=True)
    return x * lax.rsqrt(ms + _NORM_EPS) * g


def _sigmoid(x):
    return 1.0 / (1.0 + jnp.exp(-x))


def _row_permutation(n_t, bsz, to_time_major):
    n = n_t * bsz
    r = lax.broadcasted_iota(jnp.int32, (n, n), 0)
    c = lax.broadcasted_iota(jnp.int32, (n, n), 1)
    if to_time_major:
        src = lax.rem(r, bsz) * n_t + lax.div(r, bsz)
    else:
        src = lax.rem(r, n_t) * bsz + lax.div(r, n_t)
    return jnp.where(c == src, 1.0, 0.0).astype(_BF16)


def _prenorm_matmul_kernel(x_ref, g_ref, w_ref, o_ref, h_ref, *stage):
    @pl.when(pl.program_id(1) == 0)
    def _():
        if not stage:
            h_ref[...] = _rms_norm(x_ref[...], g_ref[...]).astype(h_ref.dtype)
            return
        (hb_ref,) = stage
        bsz, ts, _ = x_ref.shape
        for b in range(bsz):
            hb_ref[b] = _rms_norm(x_ref[b], g_ref[...]).astype(hb_ref.dtype)
        step = _tile(ts, _PERM_STEPS)
        perm = _row_permutation(step, bsz, True)
        for s in range(ts // step):
            slab = jnp.concatenate(
                [hb_ref[b, s * step:(s + 1) * step, :] for b in range(bsz)], axis=0)
            h_ref[s * step * bsz:(s + 1) * step * bsz, :] = jnp.dot(
                perm, slab, preferred_element_type=_F32).astype(h_ref.dtype)

    o_ref[...] = jnp.dot(
        h_ref[...], w_ref[...], preferred_element_type=_F32
    ).astype(o_ref.dtype)


def _prenorm_matmul(x, g, w, layer, *, out_dtype, to_time_major, tm=1024, tn=1024):
    bsz, seq, d = x.shape
    n = w.shape[2]
    m = bsz * seq
    tm, tn = _tile(m, tm), _tile(n, tn)
    if to_time_major:
        ts = tm // bsz
        x_arg = x
        x_spec = pl.BlockSpec((bsz, ts, d), lambda i, j: (0, i, 0))
        scratch = [pltpu.VMEM((tm, d), _BF16), pltpu.VMEM((bsz, ts, d), _BF16)]
    else:
        x_arg = x.reshape(m, d)
        x_spec = pl.BlockSpec((tm, d), lambda i, j: (i, 0))
        scratch = [pltpu.VMEM((tm, d), _BF16)]
    return pl.pallas_call(
        _prenorm_matmul_kernel,
        grid=(m // tm, n // tn),
        in_specs=[
            x_spec,
            pl.BlockSpec((1, d), lambda i, j: (0, 0)),
            pl.BlockSpec((None, d, tn), lambda i, j: (layer, 0, j)),
        ],
        out_specs=pl.BlockSpec((tm, tn), lambda i, j: (i, j)),
        out_shape=jax.ShapeDtypeStruct((m, n), out_dtype),
        scratch_shapes=scratch,
        compiler_params=_compiler_params(2),
        name="prenorm_matmul",
    )(x_arg, g, w)


def _matmul_postnorm_kernel(*refs, n_pairs):
    a_refs = refs[:n_pairs]
    w_refs = refs[n_pairs:2 * n_pairs]
    g_ref, res_ref, o_ref = refs[2 * n_pairs:2 * n_pairs + 3]
    stage = refs[2 * n_pairs + 3:]
    acc = None
    for k, (a_ref, w_ref) in enumerate(zip(a_refs, w_refs)):
        if not stage:
            a = a_ref[...].astype(_BF16)
        else:
            ab_ref = stage[k]
            bsz, ts, width = ab_ref.shape
            step = _tile(ts, _PERM_STEPS)
            perm = _row_permutation(step, bsz, False)
            for s in range(ts // step):
                slab = a_ref[s * step * bsz:(s + 1) * step * bsz, :].astype(_BF16)
                moved = jnp.dot(perm, slab, preferred_element_type=_F32)
                for b in range(bsz):
                    ab_ref[b, s * step:(s + 1) * step, :] = (
                        moved[b * step:(b + 1) * step].astype(_BF16))
            a = ab_ref[...].reshape(bsz * ts, width)
        part = jnp.dot(a, w_ref[...], preferred_element_type=_F32)
        acc = part if acc is None else acc + part
    normed = _rms_norm(acc, g_ref[...])
    o_ref[...] = res_ref[...] + normed.reshape(res_ref.shape)


def _matmul_postnorm(a_list, w, layer, g, res, *, from_time_major, tm=512):
    bsz, seq, d = res.shape
    m = bsz * seq
    tm = _tile(m, tm)
    widths = [a.shape[1] for a in a_list]
    assert len(set(widths)) == 1 and sum(widths) == w.shape[1]
    width = widths[0]
    whole = lambda i: (0, 0)
    if from_time_major:
        ts = tm // bsz
        res_arg = res
        res_spec = pl.BlockSpec((bsz, ts, d), lambda i: (0, i, 0))
        scratch = [pltpu.VMEM((bsz, ts, width), _BF16) for _ in a_list]
    else:
        res_arg = res.reshape(m, d)
        res_spec = pl.BlockSpec((tm, d), lambda i: (i, 0))
        scratch = []
    out = pl.pallas_call(
        functools.partial(_matmul_postnorm_kernel, n_pairs=len(a_list)),
        grid=(m // tm,),
        in_specs=(
            [pl.BlockSpec((tm, width), lambda i: (i, 0)) for _ in a_list]
            + [pl.BlockSpec((None, width, d), lambda i, k=k: (layer, k, 0))
               for k in range(len(a_list))]
            + [pl.BlockSpec((1, d), whole), res_spec]
        ),
        out_specs=res_spec,
        out_shape=jax.ShapeDtypeStruct(res_arg.shape, _F32),
        scratch_shapes=scratch,
        compiler_params=_compiler_params(1),
        name="matmul_postnorm",
    )(*a_list, *([w] * len(a_list)), g, res_arg)
    return out.reshape(bsz, seq, d)


def _mlp_kernel(x_ref, g_in_ref, wu_ref, wd_ref, g_out_ref, o_ref, h_ref):
    f = pl.program_id(1)

    @pl.when(f == 0)
    def _():
        h_ref[...] = _rms_norm(x_ref[...], g_in_ref[...]).astype(h_ref.dtype)
        o_ref[...] = jnp.zeros_like(o_ref)

    u = jnp.dot(h_ref[...], wu_ref[...], preferred_element_type=_F32)
    u = jnp.maximum(u, 0.0)
    o_ref[...] += jnp.dot((u * u).astype(_BF16), wd_ref[...],
                          preferred_element_type=_F32)

    @pl.when(f == pl.num_programs(1) - 1)
    def _():
        o_ref[...] = x_ref[...] + _rms_norm(o_ref[...], g_out_ref[...])


def _mlp(x, g_in, w_up, w_down, layer, g_out, *, tm=512, tf=1024):
    bsz, seq, d = x.shape
    m = bsz * seq
    d_ff = w_up.shape[2]
    tm, tf = _tile(m, tm), _tile(d_ff, tf)
    return pl.pallas_call(
        _mlp_kernel,
        grid=(m // tm, d_ff // tf),
        in_specs=[
            pl.BlockSpec((tm, d), lambda i, f: (i, 0)),
            pl.BlockSpec((1, d), lambda i, f: (0, 0)),
            pl.BlockSpec((None, d, tf), lambda i, f: (layer, 0, f)),
            pl.BlockSpec((None, tf, d), lambda i, f: (layer, f, 0)),
            pl.BlockSpec((1, d), lambda i, f: (0, 0)),
        ],
        out_specs=pl.BlockSpec((tm, d), lambda i, f: (i, 0)),
        out_shape=jax.ShapeDtypeStruct((m, d), _F32),
        scratch_shapes=[pltpu.VMEM((tm, d), _BF16)],
        compiler_params=_compiler_params(2),
        name="sq_relu_mlp",
    )(x.reshape(m, d), g_in, w_up, w_down, g_out).reshape(bsz, seq, d)


def _mixer_kernel(bg_ref, cg_ref, ax_ref, gate_ref, bx_ref,
                  ca_ref, cb_ref, cbb_ref,
                  wa_ref, ba_ref, wx_ref, bxb_ref, lam_ref,
                  ya_ref, yb_ref,
                  u_ref, xs_ref, h_ref, *, tc):
    ts, bsz, cb = bg_ref.shape
    ka, kb = _CONV_A_WIDTH, _CONV_B_WIDTH

    @pl.when(pl.program_id(1) == 0)
    def _():
        u_ref[0:ka - 1] = jnp.zeros((ka - 1, bsz, cb), _F32)
        xs_ref[0:kb - 1] = jnp.zeros((kb - 1, bsz, cb), _F32)
        h_ref[...] = jnp.zeros((bsz, cb), _F32)

    c_log_sig = _LRU_C * jax.nn.log_sigmoid(lam_ref[...])

    def chunk(s, h):
        t0 = pl.multiple_of(s * tc, tc)
        cur = pl.ds(t0, tc)
        out_rows = pl.ds(pl.multiple_of(t0 * bsz, tc * bsz), tc * bsz)

        u_ref[pl.ds(t0 + ka - 1, tc)] = cg_ref[cur] * ax_ref[cur]
        conv = ca_ref[ka - 1] * u_ref[pl.ds(t0 + ka - 1, tc)]
        for k in range(ka - 1):
            conv = conv + ca_ref[k] * u_ref[pl.ds(t0 + k, tc)]
        ya_ref[out_rows] = (bg_ref[cur] * conv).reshape(tc * bsz, cb).astype(
            ya_ref.dtype)

        xs_ref[pl.ds(t0 + kb - 1, tc)] = bx_ref[cur]
        xr = cb_ref[kb - 1] * xs_ref[pl.ds(t0 + kb - 1, tc)]
        for k in range(kb - 1):
            xr = xr + cb_ref[k] * xs_ref[pl.ds(t0 + k, tc)]
        xr = (xr + cbb_ref[...]).reshape(tc * bsz, cb)
        xr_bf = xr.astype(_BF16)
        r = _sigmoid(jnp.dot(xr_bf, wa_ref[...], preferred_element_type=_F32)
                     + ba_ref[...])
        gate_i = _sigmoid(jnp.dot(xr_bf, wx_ref[...], preferred_element_type=_F32)
                          + bxb_ref[...])
        log_a = c_log_sig * r
        a = jnp.exp(log_a).reshape(tc, bsz, cb)
        tanh_log_a = jnp.tanh(log_a)
        mult = jnp.sqrt(-2.0 * tanh_log_a / (1.0 - tanh_log_a))
        b = (mult * (gate_i * xr)).reshape(tc, bsz, cb)
        gelu_gate = jax.nn.gelu(gate_ref[cur], approximate=True)
        states = []
        for t in range(tc):
            h = a[t] * h + b[t]
            states.append(h)
        yb_ref[out_rows] = (jnp.stack(states) * gelu_gate).reshape(
            tc * bsz, cb).astype(yb_ref.dtype)
        return h

    h_ref[...] = lax.fori_loop(0, ts // tc, chunk, h_ref[...])
    u_ref[0:ka - 1] = u_ref[ts:ts + ka - 1]
    xs_ref[0:kb - 1] = xs_ref[ts:ts + kb - 1]


def _block_diag(w, per):
    h, d, _ = w.shape
    w4 = w.reshape(h // per, per, d, d)
    eye = jnp.eye(per, dtype=w.dtype)
    return jnp.einsum("cpij,pq->cpiqj", w4, eye).reshape(h // per, per * d, per * d)


def _mixer(proj, conv_a, conv_b, conv_b_bias, rg_w_a, rg_b_a, rg_w_x, rg_b_x,
           rg_lambda, *, ts=512, cb=128, tc=32):
    seq, bsz, _ = proj.shape
    d_a = conv_a.shape[1]
    d_b = conv_b.shape[1]
    assert d_a == d_b
    ts = _tile(seq, ts)
    nc = d_a // cb
    per = cb // _B_HEAD_DIM
    wa = _block_diag(rg_w_a, per).astype(_BF16)
    wx = _block_diag(rg_w_x, per).astype(_BF16)

    def seg_spec(seg):
        return pl.BlockSpec((ts, bsz, cb), lambda c, i: (i, 0, seg * nc + c))

    row_spec = pl.BlockSpec((1, cb), lambda c, i: (0, c))
    gate_w_spec = pl.BlockSpec((None, cb, cb), lambda c, i: (c, 0, 0))
    out_spec = pl.BlockSpec((ts * bsz, cb), lambda c, i: (i, c))
    return pl.pallas_call(
        functools.partial(_mixer_kernel, tc=tc),
        grid=(nc, seq // ts),
        in_specs=[seg_spec(k) for k in range(5)] + [
            pl.BlockSpec((_CONV_A_WIDTH, 1, cb), lambda c, i: (0, 0, c)),
            pl.BlockSpec((_CONV_B_WIDTH, 1, cb), lambda c, i: (0, 0, c)),
            row_spec, gate_w_spec, row_spec, gate_w_spec, row_spec, row_spec,
        ],
        out_specs=[out_spec, out_spec],
        out_shape=[jax.ShapeDtypeStruct((seq * bsz, d_a), _BF16),
                   jax.ShapeDtypeStruct((seq * bsz, d_b), _BF16)],
        scratch_shapes=[
            pltpu.VMEM((ts + _CONV_A_WIDTH - 1, bsz, cb), _F32),
            pltpu.VMEM((ts + _CONV_B_WIDTH - 1, bsz, cb), _F32),
            pltpu.VMEM((bsz, cb), _F32),
        ],
        compiler_params=_compiler_params(2),
        name="conv_rglru_mixer",
    )(proj, proj, proj, proj, proj,
      conv_a.reshape(_CONV_A_WIDTH, 1, d_a), conv_b.reshape(_CONV_B_WIDTH, 1, d_b),
      conv_b_bias.reshape(1, d_b), wa, rg_b_a.reshape(1, d_b), wx,
      rg_b_x.reshape(1, d_b), rg_lambda.reshape(1, d_b))


def _attention_kernel(q_ref, k_ref, v_ref, o_ref, acc_ref, r_ref, *, tb):
    seq = q_ref.shape[0]
    dh = _SB_HEAD_DIM
    heads = range(q_ref.shape[1] // dh)
    row = lax.broadcasted_iota(jnp.int32, (tb, tb), 0)
    col = lax.broadcasted_iota(jnp.int32, (tb, tb), 1)
    causal = col < row
    later = jnp.where(row > col, 1.0, 0.0).astype(_BF16)

    def scores(h, q0, j0, nb):
        lanes = slice(h * dh, (h + 1) * dh)
        return lax.dot_general(
            q_ref[pl.ds(q0, tb), lanes], k_ref[pl.ds(j0, nb * tb), lanes],
            (((1,), (1,)), ((), ())), preferred_element_type=_F32)

    def log_not_beta(z, valid):
        neg_z = -z
        out = (jnp.minimum(neg_z, 0.0)
               - jnp.log(1.0 + jnp.exp(jnp.minimum(z, neg_z))))
        return out if valid is None else jnp.where(valid, out, 0.0)

    def block_suffix(log_not, nb):
        stacked = jnp.concatenate(
            [log_not[:, i * tb:(i + 1) * tb] for i in range(nb)], axis=0)
        hi = stacked.astype(_BF16)
        lo = (stacked - hi.astype(_F32)).astype(_BF16)
        return (jnp.dot(hi, later, preferred_element_type=_F32)
                + jnp.dot(lo, later, preferred_element_type=_F32))

    def weights(h, z, log_not, suffix, nb, valid):
        nearer = r_ref[h]
        log_w = [None] * nb
        for i in reversed(range(nb)):
            cols = slice(i * tb, (i + 1) * tb)
            log_w[i] = z[:, cols] + log_not[:, cols] + suffix[cols] + nearer
            nearer = nearer + jnp.sum(log_not[:, cols], axis=1, keepdims=True)
        r_ref[h] = nearer
        w = jnp.exp(jnp.concatenate(log_w, axis=1))
        return w if valid is None else jnp.where(valid, w, 0.0)

    def accumulate(h, w, j0, nb):
        lanes = slice(h * dh, (h + 1) * dh)
        acc_ref[:, lanes] += jnp.dot(
            w.astype(_BF16), v_ref[pl.ds(j0, nb * tb), lanes],
            preferred_element_type=_F32)

    def key_window(q0, j0, nb, ends_on_diagonal):
        valid = None
        if ends_on_diagonal:
            valid = causal if nb == 1 else jnp.concatenate(
                [jnp.ones((tb, (nb - 1) * tb), jnp.bool_), causal], axis=1)
        z = [scores(h, q0, j0, nb) for h in heads]
        log_not = [log_not_beta(z[h], valid) for h in heads]
        suffix = [block_suffix(log_not[h], nb) for h in heads]
        w = [weights(h, z[h], log_not[h], suffix[h], nb, valid) for h in heads]
        for h in heads:
            accumulate(h, w[h], j0, nb)

    def start_block():
        acc_ref[...] = jnp.zeros_like(acc_ref)
        r_ref[...] = jnp.zeros_like(r_ref)

    def finish_block(q0):
        o_ref[pl.ds(q0, tb), :] = acc_ref[...].astype(o_ref.dtype)

    def query_block(qi, carry):
        q0 = pl.multiple_of(qi * tb, tb)
        start_block()
        key_window(q0, q0 - tb, 2, True)

        def more(c):
            n, r_max = c
            return jnp.logical_and(n < qi - 1, r_max > _F32_EXP_UNDERFLOW)

        def earlier(c):
            n, _ = c
            key_window(q0, pl.multiple_of((qi - 2 - n) * tb, tb), 1, False)
            return n + 1, jnp.max(r_ref[...])

        lax.while_loop(more, earlier, (0, jnp.max(r_ref[...])))
        finish_block(q0)
        return carry

    start_block()
    key_window(0, 0, 1, True)
    finish_block(0)
    lax.fori_loop(1, seq // tb, query_block, 0)


def _attention(qkv, *, bsz, seq, n_heads, tb=256, heads_per_step=4):
    dh = _SB_HEAD_DIM
    tb = _tile(seq, tb)
    hp = _tile(n_heads, heads_per_step)
    ng = n_heads // hp
    blk = (seq, hp * dh)
    return pl.pallas_call(
        functools.partial(_attention_kernel, tb=tb),
        grid=(bsz, ng),
        in_specs=[
            pl.BlockSpec(blk, lambda b, g: (b, g)),
            pl.BlockSpec(blk, lambda b, g: (b, ng + g)),
            pl.BlockSpec(blk, lambda b, g: (b, 2 * ng + g)),
        ],
        out_specs=pl.BlockSpec(blk, lambda b, g: (b, g)),
        out_shape=jax.ShapeDtypeStruct((bsz * seq, n_heads * dh), _BF16),
        scratch_shapes=[pltpu.VMEM((tb, hp * dh), _F32),
                        pltpu.VMEM((hp, tb, 1), _F32)],
        compiler_params=_compiler_params(2),
        name="stick_breaking_attention",
    )(qkv, qkv, qkv)


def kernel(x, norm_gains, hyb_w_in, hyb_conv_a, hyb_conv_b, hyb_conv_b_bias,
           hyb_rg_w_a, hyb_rg_b_a, hyb_rg_w_x, hyb_rg_b_x, hyb_rg_lambda,
           hyb_w_out, sb_w_qkv, sb_w_o, mlp_w_up, mlp_w_down):
    bsz, seq, d = x.shape
    depth = norm_gains.shape[0]
    n_heads = d // _SB_HEAD_DIM
    w_in = hyb_w_in.astype(_BF16)
    w_out = hyb_w_out.astype(_BF16)
    col_scale = jnp.concatenate([
        jnp.full((d,), 1.0 / math.sqrt(_SB_HEAD_DIM), _F32),
        jnp.ones((2 * d,), _F32)])
    w_qkv = (sb_w_qkv * col_scale).astype(_BF16)
    w_o = sb_w_o.astype(_BF16)
    w_up = mlp_w_up.astype(_BF16)
    w_down = mlp_w_down.astype(_BF16)
    for layer in range(depth):
        g = norm_gains[layer]
        if layer % 2 == 0:
            e = layer // 2
            proj = _prenorm_matmul(x, g[0:1], w_in, e, out_dtype=_F32,
                                   to_time_major=True)
            y_a, y_b = _mixer(proj.reshape(seq, bsz, -1), hyb_conv_a[e],
                              hyb_conv_b[e], hyb_conv_b_bias[e], hyb_rg_w_a[e],
                              hyb_rg_b_a[e], hyb_rg_w_x[e], hyb_rg_b_x[e],
                              hyb_rg_lambda[e])
            x = _matmul_postnorm([y_a, y_b], w_out, e, g[1:2], x,
                                 from_time_major=True)
        else:
            o = layer // 2
            qkv = _prenorm_matmul(x, g[0:1], w_qkv, o, out_dtype=_BF16,
                                  to_time_major=False)
            att = _attention(qkv, bsz=bsz, seq=seq, n_heads=n_heads)
            x = _matmul_postnorm([att], w_o, o, g[1:2], x, from_time_major=False)
        x = _mlp(x, g[2:3], w_up, w_down, layer, g[3:4])
    return x
```

```python
import functools
import math

import jax
import jax.numpy as jnp
from jax import lax
from jax.experimental import pallas as pl
from jax.experimental.pallas import tpu as pltpu

_NORM_EPS = 1e-6
_LRU_C = 8.0
_B_HEAD_DIM = 64
_SB_HEAD_DIM = 128
_CONV_A_WIDTH = 3
_CONV_B_WIDTH = 4
_PERM_STEPS = 32
_LOG2_E = math.log2(math.e)
_NORM_GROUP_ROWS = 8
_F32_EXP2_UNDERFLOW = 151.0

_VMEM_LIMIT_BYTES = 56 * 1024 * 1024

_BF16 = jnp.bfloat16
_F32 = jnp.float32


def _compiler_params(n_grid_axes):
    return pltpu.CompilerParams(
        dimension_semantics=("arbitrary",) * n_grid_axes,
        vmem_limit_bytes=_VMEM_LIMIT_BYTES,
    )


def _tile(dim, target):
    t = min(dim, target)
    while dim % t:
        t //= 2
    return t


def _rms_norm(x, g):
    ms = jnp.mean(x * x, axis=-1, keepdims=True)
    return x * lax.rsqrt(ms + _NORM_EPS) * g


def _sigmoid(x):
    return 1.0 / (1.0 + jnp.exp(-x))


def _row_permutation(n_t, bsz, to_time_major):
    n = n_t * bsz
    r = lax.broadcasted_iota(jnp.int32, (n, n), 0)
    c = lax.broadcasted_iota(jnp.int32, (n, n), 1)
    if to_time_major:
        src = lax.rem(r, bsz) * n_t + lax.div(r, bsz)
    else:
        src = lax.rem(r, n_t) * bsz + lax.div(r, n_t)
    return jnp.where(c == src, 1.0, 0.0).astype(_BF16)


def _prenorm_matmul_kernel(x_ref, g_ref, w_ref, o_ref, h_ref, *stage):
    def normalise():
        if not stage:
            h_ref[...] = _rms_norm(x_ref[...], g_ref[...]).astype(h_ref.dtype)
            return
        (hb_ref,) = stage
        bsz, ts, _ = x_ref.shape
        for b in range(bsz):
            hb_ref[b] = _rms_norm(x_ref[b], g_ref[...]).astype(hb_ref.dtype)
        step = _tile(ts, _PERM_STEPS)
        perm = _row_permutation(step, bsz, True)
        for s in range(ts // step):
            slab = jnp.concatenate(
                [hb_ref[b, s * step:(s + 1) * step, :] for b in range(bsz)], axis=0)
            h_ref[s * step * bsz:(s + 1) * step * bsz, :] = jnp.dot(
                perm, slab, preferred_element_type=_F32).astype(h_ref.dtype)

    def project():
        o_ref[...] = jnp.dot(
            h_ref[...], w_ref[...], preferred_element_type=_F32
        ).astype(o_ref.dtype)

    @pl.when(pl.program_id(1) == 0)
    def _():
        normalise()
        project()

    @pl.when(pl.program_id(1) > 0)
    def _():
        project()


def _prenorm_matmul(x, g, w, layer, *, out_dtype, to_time_major, tm=1024, tn=1024):
    bsz, seq, d = x.shape
    n = w.shape[2]
    m = bsz * seq
    tm, tn = _tile(m, tm), _tile(n, tn)
    if to_time_major:
        ts = tm // bsz
        x_arg = x
        x_spec = pl.BlockSpec((bsz, ts, d), lambda i, j: (0, i, 0))
        scratch = [pltpu.VMEM((tm, d), _BF16), pltpu.VMEM((bsz, ts, d), _BF16)]
    else:
        x_arg = x.reshape(m, d)
        x_spec = pl.BlockSpec((tm, d), lambda i, j: (i, 0))
        scratch = [pltpu.VMEM((tm, d), _BF16)]
    return pl.pallas_call(
        _prenorm_matmul_kernel,
        grid=(m // tm, n // tn),
        in_specs=[
            x_spec,
            pl.BlockSpec((1, d), lambda i, j: (0, 0)),
            pl.BlockSpec((None, d, tn), lambda i, j: (layer, 0, j)),
        ],
        out_specs=pl.BlockSpec((tm, tn), lambda i, j: (i, j)),
        out_shape=jax.ShapeDtypeStruct((m, n), out_dtype),
        scratch_shapes=scratch,
        compiler_params=_compiler_params(2),
        name="prenorm_matmul",
    )(x_arg, g, w)


def _matmul_postnorm_kernel(*refs, n_pairs):
    a_refs = refs[:n_pairs]
    w_refs = refs[n_pairs:2 * n_pairs]
    g_ref, res_ref, o_ref = refs[2 * n_pairs:2 * n_pairs + 3]
    stage = refs[2 * n_pairs + 3:]
    acc = None
    for k, (a_ref, w_ref) in enumerate(zip(a_refs, w_refs)):
        if not stage:
            a = a_ref[...].astype(_BF16)
        else:
            ab_ref = stage[k]
            bsz, ts, width = ab_ref.shape
            step = _tile(ts, _PERM_STEPS)
            perm = _row_permutation(step, bsz, False)
            for s in range(ts // step):
                slab = a_ref[s * step * bsz:(s + 1) * step * bsz, :].astype(_BF16)
                moved = jnp.dot(perm, slab, preferred_element_type=_F32)
                for b in range(bsz):
                    ab_ref[b, s * step:(s + 1) * step, :] = (
                        moved[b * step:(b + 1) * step].astype(_BF16))
            a = ab_ref[...].reshape(bsz * ts, width)
        part = jnp.dot(a, w_ref[...], preferred_element_type=_F32)
        acc = part if acc is None else acc + part
    normed = _rms_norm(acc, g_ref[...])
    o_ref[...] = res_ref[...] + normed.reshape(res_ref.shape)


def _matmul_postnorm(a_list, w, layer, g, res, *, from_time_major, tm=512):
    bsz, seq, d = res.shape
    m = bsz * seq
    tm = _tile(m, tm)
    widths = [a.shape[1] for a in a_list]
    assert len(set(widths)) == 1 and sum(widths) == w.shape[1]
    width = widths[0]
    whole = lambda i: (0, 0)
    if from_time_major:
        ts = tm // bsz
        res_arg = res
        res_spec = pl.BlockSpec((bsz, ts, d), lambda i: (0, i, 0))
        scratch = [pltpu.VMEM((bsz, ts, width), _BF16) for _ in a_list]
    else:
        res_arg = res.reshape(m, d)
        res_spec = pl.BlockSpec((tm, d), lambda i: (i, 0))
        scratch = []
    out = pl.pallas_call(
        functools.partial(_matmul_postnorm_kernel, n_pairs=len(a_list)),
        grid=(m // tm,),
        in_specs=(
            [pl.BlockSpec((tm, width), lambda i: (i, 0)) for _ in a_list]
            + [pl.BlockSpec((None, width, d), lambda i, k=k: (layer, k, 0))
               for k in range(len(a_list))]
            + [pl.BlockSpec((1, d), whole), res_spec]
        ),
        out_specs=res_spec,
        out_shape=jax.ShapeDtypeStruct(res_arg.shape, _F32),
        scratch_shapes=scratch,
        compiler_params=_compiler_params(1),
        name="matmul_postnorm",
    )(*a_list, *([w] * len(a_list)), g, res_arg)
    return out.reshape(bsz, seq, d)


def _mlp_kernel(x_ref, g_in_ref, wu_ref, wd_ref, g_out_ref, o_ref, h_ref):
    f = pl.program_id(1)
    last = pl.num_programs(1) - 1

    def accumulate():
        u = jnp.dot(h_ref[...], wu_ref[...], preferred_element_type=_F32)
        u = jnp.maximum(u, 0.0)
        o_ref[...] += jnp.dot((u * u).astype(_BF16), wd_ref[...],
                              preferred_element_type=_F32)

    def begin():
        h_ref[...] = _rms_norm(x_ref[...], g_in_ref[...]).astype(h_ref.dtype)
        o_ref[...] = jnp.zeros_like(o_ref)

    def finish():
        rows = _tile(o_ref.shape[0], _NORM_GROUP_ROWS)
        for c in range(o_ref.shape[0] // rows):
            grp = slice(c * rows, (c + 1) * rows)
            o_ref[grp, :] = x_ref[grp, :] + _rms_norm(o_ref[grp, :], g_out_ref[...])

    @pl.when(f == 0)
    def _():
        begin()
        accumulate()

    @pl.when(jnp.logical_and(f > 0, f < last))
    def _():
        accumulate()

    @pl.when(f == last)
    def _():
        accumulate()
        finish()


def _mlp(x, g_in, w_up, w_down, layer, g_out, *, tm=512, tf=1024):
    bsz, seq, d = x.shape
    m = bsz * seq
    d_ff = w_up.shape[2]
    tm, tf = _tile(m, tm), _tile(d_ff, tf)
    assert d_ff // tf >= 2, "the kernel's first and last hidden steps are distinct"
    return pl.pallas_call(
        _mlp_kernel,
        grid=(m // tm, d_ff // tf),
        in_specs=[
            pl.BlockSpec((tm, d), lambda i, f: (i, 0)),
            pl.BlockSpec((1, d), lambda i, f: (0, 0)),
            pl.BlockSpec((None, d, tf), lambda i, f: (layer, 0, f)),
            pl.BlockSpec((None, tf, d), lambda i, f: (layer, f, 0)),
            pl.BlockSpec((1, d), lambda i, f: (0, 0)),
        ],
        out_specs=pl.BlockSpec((tm, d), lambda i, f: (i, 0)),
        out_shape=jax.ShapeDtypeStruct((m, d), _F32),
        scratch_shapes=[pltpu.VMEM((tm, d), _BF16)],
        compiler_params=_compiler_params(2),
        name="sq_relu_mlp",
    )(x.reshape(m, d), g_in, w_up, w_down, g_out).reshape(bsz, seq, d)


def _mixer_kernel(bg_ref, cg_ref, ax_ref, gate_ref, bx_ref,
                  ca_ref, cb_ref, cbb_ref,
                  wa_ref, ba_ref, wx_ref, bxb_ref, lam_ref,
                  ya_ref, yb_ref,
                  u_ref, xs_ref, h_ref, *, tc):
    ts, bsz, cb = bg_ref.shape
    ka, kb = _CONV_A_WIDTH, _CONV_B_WIDTH

    @pl.when(pl.program_id(1) == 0)
    def _():
        u_ref[0:ka - 1] = jnp.zeros((ka - 1, bsz, cb), _F32)
        xs_ref[0:kb - 1] = jnp.zeros((kb - 1, bsz, cb), _F32)
        h_ref[...] = jnp.zeros((bsz, cb), _F32)

    c_log_sig = _LRU_C * jax.nn.log_sigmoid(lam_ref[...])

    def chunk(s, h):
        t0 = pl.multiple_of(s * tc, tc)
        cur = pl.ds(t0, tc)
        out_rows = pl.ds(pl.multiple_of(t0 * bsz, tc * bsz), tc * bsz)

        u_ref[pl.ds(t0 + ka - 1, tc)] = cg_ref[cur] * ax_ref[cur]
        conv = ca_ref[ka - 1] * u_ref[pl.ds(t0 + ka - 1, tc)]
        for k in range(ka - 1):
            conv = conv + ca_ref[k] * u_ref[pl.ds(t0 + k, tc)]
        ya_ref[out_rows] = (bg_ref[cur] * conv).reshape(tc * bsz, cb).astype(
            ya_ref.dtype)

        xs_ref[pl.ds(t0 + kb - 1, tc)] = bx_ref[cur]
        xr = cb_ref[kb - 1] * xs_ref[pl.ds(t0 + kb - 1, tc)]
        for k in range(kb - 1):
            xr = xr + cb_ref[k] * xs_ref[pl.ds(t0 + k, tc)]
        xr = (xr + cbb_ref[...]).reshape(tc * bsz, cb)
        xr_bf = xr.astype(_BF16)
        r = _sigmoid(jnp.dot(xr_bf, wa_ref[...], preferred_element_type=_F32)
                     + ba_ref[...])
        gate_i = _sigmoid(jnp.dot(xr_bf, wx_ref[...], preferred_element_type=_F32)
                          + bxb_ref[...])
        log_a = c_log_sig * r
        a = jnp.exp(log_a).reshape(tc, bsz, cb)
        tanh_log_a = jnp.tanh(log_a)
        mult = jnp.sqrt(-2.0 * tanh_log_a / (1.0 - tanh_log_a))
        b = (mult * (gate_i * xr)).reshape(tc, bsz, cb)
        gelu_gate = jax.nn.gelu(gate_ref[cur], approximate=True)
        states = []
        for t in range(tc):
            h = a[t] * h + b[t]
            states.append(h)
        yb_ref[out_rows] = (jnp.stack(states) * gelu_gate).reshape(
            tc * bsz, cb).astype(yb_ref.dtype)
        return h

    h_ref[...] = lax.fori_loop(0, ts // tc, chunk, h_ref[...])
    u_ref[0:ka - 1] = u_ref[ts:ts + ka - 1]
    xs_ref[0:kb - 1] = xs_ref[ts:ts + kb - 1]


def _block_diag(w, per):
    h, d, _ = w.shape
    w4 = w.reshape(h // per, per, d, d)
    eye = jnp.eye(per, dtype=w.dtype)
    return jnp.einsum("cpij,pq->cpiqj", w4, eye).reshape(h // per, per * d, per * d)


def _mixer(proj, conv_a, conv_b, conv_b_bias, rg_w_a, rg_b_a, rg_w_x, rg_b_x,
           rg_lambda, *, ts=512, cb=128, tc=32):
    seq, bsz, _ = proj.shape
    d_a = conv_a.shape[1]
    d_b = conv_b.shape[1]
    assert d_a == d_b
    ts = _tile(seq, ts)
    nc = d_a // cb
    per = cb // _B_HEAD_DIM
    wa = _block_diag(rg_w_a, per).astype(_BF16)
    wx = _block_diag(rg_w_x, per).astype(_BF16)

    def seg_spec(seg):
        return pl.BlockSpec((ts, bsz, cb), lambda c, i: (i, 0, seg * nc + c))

    row_spec = pl.BlockSpec((1, cb), lambda c, i: (0, c))
    gate_w_spec = pl.BlockSpec((None, cb, cb), lambda c, i: (c, 0, 0))
    out_spec = pl.BlockSpec((ts * bsz, cb), lambda c, i: (i, c))
    return pl.pallas_call(
        functools.partial(_mixer_kernel, tc=tc),
        grid=(nc, seq // ts),
        in_specs=[seg_spec(k) for k in range(5)] + [
            pl.BlockSpec((_CONV_A_WIDTH, 1, cb), lambda c, i: (0, 0, c)),
            pl.BlockSpec((_CONV_B_WIDTH, 1, cb), lambda c, i: (0, 0, c)),
            row_spec, gate_w_spec, row_spec, gate_w_spec, row_spec, row_spec,
        ],
        out_specs=[out_spec, out_spec],
        out_shape=[jax.ShapeDtypeStruct((seq * bsz, d_a), _BF16),
                   jax.ShapeDtypeStruct((seq * bsz, d_b), _BF16)],
        scratch_shapes=[
            pltpu.VMEM((ts + _CONV_A_WIDTH - 1, bsz, cb), _F32),
            pltpu.VMEM((ts + _CONV_B_WIDTH - 1, bsz, cb), _F32),
            pltpu.VMEM((bsz, cb), _F32),
        ],
        compiler_params=_compiler_params(2),
        name="conv_rglru_mixer",
    )(proj, proj, proj, proj, proj,
      conv_a.reshape(_CONV_A_WIDTH, 1, d_a), conv_b.reshape(_CONV_B_WIDTH, 1, d_b),
      conv_b_bias.reshape(1, d_b), wa, rg_b_a.reshape(1, d_b), wx,
      rg_b_x.reshape(1, d_b), rg_lambda.reshape(1, d_b))


def _attention_kernel(q_ref, k_ref, v_ref, o_ref, acc_ref, r_ref, *, tb):
    seq = q_ref.shape[0]
    dh = _SB_HEAD_DIM
    heads = range(q_ref.shape[1] // dh)
    row = lax.broadcasted_iota(jnp.int32, (tb, tb), 0)
    col = lax.broadcasted_iota(jnp.int32, (tb, tb), 1)
    causal = col < row
    later = jnp.where(row > col, 1.0, 0.0).astype(_BF16)
    later2 = jnp.concatenate([later, later], axis=0)

    def scores(h, q0, j0, nb):
        lanes = slice(h * dh, (h + 1) * dh)
        return lax.dot_general(
            q_ref[pl.ds(q0, tb), lanes], k_ref[pl.ds(j0, nb * tb), lanes],
            (((1,), (1,)), ((), ())), preferred_element_type=_F32)

    def neg_log2_not_beta(z, nb, ends_on_diagonal):
        p = (jnp.maximum(z, 0.0)
             + jnp.log(1.0 + jnp.exp2(-jnp.abs(z))) * _LOG2_E)
        blocks = [p[:, i * tb:(i + 1) * tb] for i in range(nb)]
        if ends_on_diagonal:
            blocks[-1] = jnp.where(causal, blocks[-1], 0.0)
        return blocks

    def block_suffix(blocks):
        stacked = jnp.concatenate(blocks, axis=0)
        hi = stacked.astype(_BF16)
        lo = (stacked - hi.astype(_F32)).astype(_BF16)
        return jnp.dot(jnp.concatenate([hi, lo], axis=1), later2,
                       preferred_element_type=_F32)

    def weights(h, z, blocks, suffix, ends_on_diagonal):
        nb = len(blocks)
        nearer = r_ref[h]
        w = [None] * nb
        for i in reversed(range(nb)):
            blk = slice(i * tb, (i + 1) * tb)
            w[i] = jnp.exp2(z[:, blk] - blocks[i] - suffix[blk] - nearer)
            nearer = nearer + jnp.sum(blocks[i], axis=1, keepdims=True)
        r_ref[h] = nearer
        if ends_on_diagonal:
            w[-1] = jnp.where(causal, w[-1], 0.0)
        return jnp.concatenate(w, axis=1).astype(_BF16)

    def accumulate(h, w, j0, nb):
        lanes = slice(h * dh, (h + 1) * dh)
        acc_ref[:, lanes] += jnp.dot(
            w, v_ref[pl.ds(j0, nb * tb), lanes], preferred_element_type=_F32)

    def key_window(q0, j0, nb, ends_on_diagonal):
        z = [scores(h, q0, j0, nb) for h in heads]
        p = [neg_log2_not_beta(z[h], nb, ends_on_diagonal) for h in heads]
        suffix = [block_suffix(p[h]) for h in heads]
        w = [weights(h, z[h], p[h], suffix[h], ends_on_diagonal) for h in heads]
        for h in heads:
            accumulate(h, w[h], j0, nb)

    def start_block():
        acc_ref[...] = jnp.zeros_like(acc_ref)
        r_ref[...] = jnp.zeros_like(r_ref)

    def finish_block(q0):
        o_ref[pl.ds(q0, tb), :] = acc_ref[...].astype(o_ref.dtype)

    def query_block(qi, carry):
        q0 = pl.multiple_of(qi * tb, tb)
        start_block()
        key_window(q0, q0 - tb, 2, True)

        def more(c):
            n, r_min = c
            return jnp.logical_and(n < qi - 1, r_min < _F32_EXP2_UNDERFLOW)

        def earlier(c):
            n, _ = c
            key_window(q0, pl.multiple_of((qi - 2 - n) * tb, tb), 1, False)
            return n + 1, jnp.min(r_ref[...])

        lax.while_loop(more, earlier, (0, jnp.min(r_ref[...])))
        finish_block(q0)
        return carry

    start_block()
    key_window(0, 0, 1, True)
    finish_block(0)
    lax.fori_loop(1, seq // tb, query_block, 0)


def _attention(qkv, *, bsz, seq, n_heads, tb=256, heads_per_step=4):
    dh = _SB_HEAD_DIM
    tb = _tile(seq, tb)
    hp = _tile(n_heads, heads_per_step)
    ng = n_heads // hp
    blk = (seq, hp * dh)
    return pl.pallas_call(
        functools.partial(_attention_kernel, tb=tb),
        grid=(bsz, ng),
        in_specs=[
            pl.BlockSpec(blk, lambda b, g: (b, g)),
            pl.BlockSpec(blk, lambda b, g: (b, ng + g)),
            pl.BlockSpec(blk, lambda b, g: (b, 2 * ng + g)),
        ],
        out_specs=pl.BlockSpec(blk, lambda b, g: (b, g)),
        out_shape=jax.ShapeDtypeStruct((bsz * seq, n_heads * dh), _BF16),
        scratch_shapes=[pltpu.VMEM((tb, hp * dh), _F32),
                        pltpu.VMEM((hp, tb, 1), _F32)],
        compiler_params=_compiler_params(2),
        name="stick_breaking_attention",
    )(qkv, qkv, qkv)


def kernel(x, norm_gains, hyb_w_in, hyb_conv_a, hyb_conv_b, hyb_conv_b_bias,
           hyb_rg_w_a, hyb_rg_b_a, hyb_rg_w_x, hyb_rg_b_x, hyb_rg_lambda,
           hyb_w_out, sb_w_qkv, sb_w_o, mlp_w_up, mlp_w_down):
    bsz, seq, d = x.shape
    depth = norm_gains.shape[0]
    n_heads = d // _SB_HEAD_DIM
    w_in = hyb_w_in.astype(_BF16)
    w_out = hyb_w_out.astype(_BF16)
    col_scale = jnp.concatenate([
        jnp.full((d,), _LOG2_E / math.sqrt(_SB_HEAD_DIM), _F32),
        jnp.ones((2 * d,), _F32)])
    w_qkv = (sb_w_qkv * col_scale).astype(_BF16)
    w_o = sb_w_o.astype(_BF16)
    w_up = mlp_w_up.astype(_BF16)
    w_down = mlp_w_down.astype(_BF16)
    for layer in range(depth):
        g = norm_gains[layer]
        if layer % 2 == 0:
            e = layer // 2
            proj = _prenorm_matmul(x, g[0:1], w_in, e, out_dtype=_F32,
                                   to_time_major=True)
            y_a, y_b = _mixer(proj.reshape(seq, bsz, -1), hyb_conv_a[e],
                              hyb_conv_b[e], hyb_conv_b_bias[e], hyb_rg_w_a[e],
                              hyb_rg_b_a[e], hyb_rg_w_x[e], hyb_rg_b_x[e],
                              hyb_rg_lambda[e])
            x = _matmul_postnorm([y_a, y_b], w_out, e, g[1:2], x,
                                 from_time_major=True)
        else:
            o = layer // 2
            qkv = _prenorm_matmul(x, g[0:1], w_qkv, o, out_dtype=_BF16,
                                  to_time_major=False)
            att = _attention(qkv, bsz=bsz, seq=seq, n_heads=n_heads)
            x = _matmul_postnorm([att], w_o, o, g[1:2], x, from_time_major=False)
        x = _mlp(x, g[2:3], w_up, w_down, layer, g[3:4])
    return x
```

```python
import functools
import math

import jax
import jax.numpy as jnp
from jax import lax
from jax.experimental import pallas as pl
from jax.experimental.pallas import tpu as pltpu

_NORM_EPS = 1e-6
_LRU_C = 8.0
_B_HEAD_DIM = 64
_SB_HEAD_DIM = 128
_CONV_A_WIDTH = 3
_CONV_B_WIDTH = 4
_PERM_STEPS = 32
_LOG2_E = math.log2(math.e)
_NORM_GROUP_ROWS = 8
_F32_EXP2_UNDERFLOW = 151.0

_VMEM_LIMIT_BYTES = 56 * 1024 * 1024

_BF16 = jnp.bfloat16
_F32 = jnp.float32


def _compiler_params(n_grid_axes):
    return pltpu.CompilerParams(
        dimension_semantics=("arbitrary",) * n_grid_axes,
        vmem_limit_bytes=_VMEM_LIMIT_BYTES,
    )


def _tile(dim, target):
    t = min(dim, target)
    while dim % t:
        t //= 2
    return t


def _rms_norm(x, g):
    ms = jnp.mean(x * x, axis=-1, keepdims=True)
    return x * lax.rsqrt(ms + _NORM_EPS) * g


def _row_permutation(n_t, bsz, to_time_major):
    n = n_t * bsz
    r = lax.broadcasted_iota(jnp.int32, (n, n), 0)
    c = lax.broadcasted_iota(jnp.int32, (n, n), 1)
    if to_time_major:
        src = lax.rem(r, bsz) * n_t + lax.div(r, bsz)
    else:
        src = lax.rem(r, n_t) * bsz + lax.div(r, n_t)
    return jnp.where(c == src, 1.0, 0.0).astype(_BF16)


def _prenorm_matmul_kernel(x_ref, g_ref, w_ref, o_ref, h_ref, *stage):
    def normalise():
        if not stage:
            h_ref[...] = _rms_norm(x_ref[...], g_ref[...]).astype(h_ref.dtype)
            return
        (hb_ref,) = stage
        bsz, ts, _ = x_ref.shape
        for b in range(bsz):
            hb_ref[b] = _rms_norm(x_ref[b], g_ref[...]).astype(hb_ref.dtype)
        step = _tile(ts, _PERM_STEPS)
        perm = _row_permutation(step, bsz, True)
        for s in range(ts // step):
            slab = jnp.concatenate(
                [hb_ref[b, s * step:(s + 1) * step, :] for b in range(bsz)], axis=0)
            h_ref[s * step * bsz:(s + 1) * step * bsz, :] = jnp.dot(
                perm, slab, preferred_element_type=_F32).astype(h_ref.dtype)

    def project():
        o_ref[...] = jnp.dot(
            h_ref[...], w_ref[...], preferred_element_type=_F32
        ).astype(o_ref.dtype)

    @pl.when(pl.program_id(1) == 0)
    def _():
        normalise()
        project()

    @pl.when(pl.program_id(1) > 0)
    def _():
        project()


def _prenorm_matmul(x, g, w, layer, *, out_dtype, to_time_major, tm=1024, tn=1024):
    bsz, seq, d = x.shape
    n = w.shape[2]
    m = bsz * seq
    tm, tn = _tile(m, tm), _tile(n, tn)
    n_row_tiles = m // tm

    def x_tile(i, j):
        return jnp.minimum(i + jnp.where(j > 0, 1, 0), n_row_tiles - 1)

    if to_time_major:
        ts = tm // bsz
        x_arg = x
        x_spec = pl.BlockSpec((bsz, ts, d), lambda i, j: (0, x_tile(i, j), 0))
        scratch = [pltpu.VMEM((tm, d), _BF16), pltpu.VMEM((bsz, ts, d), _BF16)]
    else:
        x_arg = x.reshape(m, d)
        x_spec = pl.BlockSpec((tm, d), lambda i, j: (x_tile(i, j), 0))
        scratch = [pltpu.VMEM((tm, d), _BF16)]
    return pl.pallas_call(
        _prenorm_matmul_kernel,
        grid=(n_row_tiles, n // tn),
        in_specs=[
            x_spec,
            pl.BlockSpec((1, d), lambda i, j: (0, 0)),
            pl.BlockSpec((None, d, tn), lambda i, j: (layer, 0, j)),
        ],
        out_specs=pl.BlockSpec((tm, tn), lambda i, j: (i, j)),
        out_shape=jax.ShapeDtypeStruct((m, n), out_dtype),
        scratch_shapes=scratch,
        compiler_params=_compiler_params(2),
        name="prenorm_matmul",
    )(x_arg, g, w)


def _matmul_postnorm_kernel(*refs, n_pairs):
    a_refs = refs[:n_pairs]
    w_refs = refs[n_pairs:2 * n_pairs]
    g_ref, res_ref, o_ref = refs[2 * n_pairs:2 * n_pairs + 3]
    stage = refs[2 * n_pairs + 3:]
    acc = None
    for k, (a_ref, w_ref) in enumerate(zip(a_refs, w_refs)):
        if not stage:
            a = a_ref[...].astype(_BF16)
        else:
            ab_ref = stage[k]
            bsz, ts, width = ab_ref.shape
            step = _tile(ts, _PERM_STEPS)
            perm = _row_permutation(step, bsz, False)
            for s in range(ts // step):
                slab = a_ref[s * step * bsz:(s + 1) * step * bsz, :].astype(_BF16)
                moved = jnp.dot(perm, slab, preferred_element_type=_F32)
                for b in range(bsz):
                    ab_ref[b, s * step:(s + 1) * step, :] = (
                        moved[b * step:(b + 1) * step].astype(_BF16))
            a = ab_ref[...].reshape(bsz * ts, width)
        part = jnp.dot(a, w_ref[...], preferred_element_type=_F32)
        acc = part if acc is None else acc + part
    normed = _rms_norm(acc, g_ref[...])
    o_ref[...] = res_ref[...] + normed.reshape(res_ref.shape)


def _matmul_postnorm(a_list, w, layer, g, res, *, from_time_major, tm=512):
    bsz, seq, d = res.shape
    m = bsz * seq
    tm = _tile(m, tm)
    widths = [a.shape[1] for a in a_list]
    assert len(set(widths)) == 1 and sum(widths) == w.shape[1]
    width = widths[0]
    whole = lambda i: (0, 0)
    if from_time_major:
        ts = tm // bsz
        res_arg = res
        res_spec = pl.BlockSpec((bsz, ts, d), lambda i: (0, i, 0))
        scratch = [pltpu.VMEM((bsz, ts, width), _BF16) for _ in a_list]
    else:
        res_arg = res.reshape(m, d)
        res_spec = pl.BlockSpec((tm, d), lambda i: (i, 0))
        scratch = []
    out = pl.pallas_call(
        functools.partial(_matmul_postnorm_kernel, n_pairs=len(a_list)),
        grid=(m // tm,),
        in_specs=(
            [pl.BlockSpec((tm, width), lambda i: (i, 0)) for _ in a_list]
            + [pl.BlockSpec((None, width, d), lambda i, k=k: (layer, k, 0))
               for k in range(len(a_list))]
            + [pl.BlockSpec((1, d), whole), res_spec]
        ),
        out_specs=res_spec,
        out_shape=jax.ShapeDtypeStruct(res_arg.shape, _F32),
        scratch_shapes=scratch,
        compiler_params=_compiler_params(1),
        name="matmul_postnorm",
    )(*a_list, *([w] * len(a_list)), g, res_arg)
    return out.reshape(bsz, seq, d)


def _mlp_kernel(x_ref, g_in_ref, wu_ref, wd_ref, g_out_ref, o_ref, h_ref):
    f = pl.program_id(1)
    last = pl.num_programs(1) - 1

    def accumulate():
        u = jnp.dot(h_ref[...], wu_ref[...], preferred_element_type=_F32)
        u = jnp.maximum(u, 0.0)
        o_ref[...] += jnp.dot((u * u).astype(_BF16), wd_ref[...],
                              preferred_element_type=_F32)

    def begin():
        h_ref[...] = _rms_norm(x_ref[...], g_in_ref[...]).astype(h_ref.dtype)
        o_ref[...] = jnp.zeros_like(o_ref)

    def finish():
        rows = _tile(o_ref.shape[0], _NORM_GROUP_ROWS)
        for c in range(o_ref.shape[0] // rows):
            grp = slice(c * rows, (c + 1) * rows)
            o_ref[grp, :] = x_ref[grp, :] + _rms_norm(o_ref[grp, :], g_out_ref[...])

    @pl.when(f == 0)
    def _():
        begin()
        accumulate()

    @pl.when(jnp.logical_and(f > 0, f < last))
    def _():
        accumulate()

    @pl.when(f == last)
    def _():
        accumulate()
        finish()


def _mlp(x, g_in, w_up, w_down, layer, g_out, *, tm=512, tf=1024):
    bsz, seq, d = x.shape
    m = bsz * seq
    d_ff = w_up.shape[2]
    tm, tf = _tile(m, tm), _tile(d_ff, tf)
    assert d_ff // tf >= 2, "the kernel's first and last hidden steps are distinct"
    return pl.pallas_call(
        _mlp_kernel,
        grid=(m // tm, d_ff // tf),
        in_specs=[
            pl.BlockSpec((tm, d), lambda i, f: (i, 0)),
            pl.BlockSpec((1, d), lambda i, f: (0, 0)),
            pl.BlockSpec((None, d, tf), lambda i, f: (layer, 0, f)),
            pl.BlockSpec((None, tf, d), lambda i, f: (layer, f, 0)),
            pl.BlockSpec((1, d), lambda i, f: (0, 0)),
        ],
        out_specs=pl.BlockSpec((tm, d), lambda i, f: (i, 0)),
        out_shape=jax.ShapeDtypeStruct((m, d), _F32),
        scratch_shapes=[pltpu.VMEM((tm, d), _BF16)],
        compiler_params=_compiler_params(2),
        name="sq_relu_mlp",
    )(x.reshape(m, d), g_in, w_up, w_down, g_out).reshape(bsz, seq, d)


def _mixer_kernel(bg_ref, cg_ref, ax_ref, gate_ref, bx_ref,
                  ca_ref, cb_ref, cbb_ref,
                  wa_ref, ba_ref, wx_ref, bxb_ref, lam_ref,
                  ya_ref, yb_ref,
                  u_ref, xs_ref, h_ref, *, tc):
    ts, bsz, cb = bg_ref.shape
    ka, kb = _CONV_A_WIDTH, _CONV_B_WIDTH

    @pl.when(pl.program_id(1) == 0)
    def _():
        u_ref[0:ka - 1] = jnp.zeros((ka - 1, bsz, cb), _F32)
        xs_ref[0:kb - 1] = jnp.zeros((kb - 1, bsz, cb), _F32)
        h_ref[...] = jnp.zeros((bsz, cb), _F32)

    c_log_sig = _LRU_C * jax.nn.log_sigmoid(lam_ref[...])

    def chunk(s, h):
        t0 = pl.multiple_of(s * tc, tc)
        cur = pl.ds(t0, tc)
        out_rows = pl.ds(pl.multiple_of(t0 * bsz, tc * bsz), tc * bsz)

        u_ref[pl.ds(t0 + ka - 1, tc)] = cg_ref[cur] * ax_ref[cur]
        conv = ca_ref[ka - 1] * u_ref[pl.ds(t0 + ka - 1, tc)]
        for k in range(ka - 1):
            conv = conv + ca_ref[k] * u_ref[pl.ds(t0 + k, tc)]
        ya_ref[out_rows] = (bg_ref[cur] * conv).reshape(tc * bsz, cb).astype(
            ya_ref.dtype)

        xs_ref[pl.ds(t0 + kb - 1, tc)] = bx_ref[cur]
        xr = cb_ref[kb - 1] * xs_ref[pl.ds(t0 + kb - 1, tc)]
        for k in range(kb - 1):
            xr = xr + cb_ref[k] * xs_ref[pl.ds(t0 + k, tc)]
        xr = (xr + cbb_ref[...]).reshape(tc * bsz, cb)
        xr_bf = xr.astype(_BF16)
        r = 1.0 / (1.0 + jnp.exp2(
            jnp.dot(xr_bf, wa_ref[...], preferred_element_type=_F32) + ba_ref[...]))
        gate_i = 1.0 / (1.0 + jnp.exp2(
            jnp.dot(xr_bf, wx_ref[...], preferred_element_type=_F32) + bxb_ref[...]))
        log_a = c_log_sig * r
        a = jnp.exp(log_a).reshape(tc, bsz, cb)
        tanh_log_a = jnp.tanh(log_a)
        mult = jnp.sqrt(-2.0 * tanh_log_a / (1.0 - tanh_log_a))
        b = (mult * (gate_i * xr)).reshape(tc, bsz, cb)
        gelu_gate = jax.nn.gelu(gate_ref[cur], approximate=True)
        states = []
        for t in range(tc):
            h = a[t] * h + b[t]
            states.append(h)
        yb_ref[out_rows] = (jnp.stack(states) * gelu_gate).reshape(
            tc * bsz, cb).astype(yb_ref.dtype)
        return h

    h_ref[...] = lax.fori_loop(0, ts // tc, chunk, h_ref[...])
    u_ref[0:ka - 1] = u_ref[ts:ts + ka - 1]
    xs_ref[0:kb - 1] = xs_ref[ts:ts + kb - 1]


def _block_diag(w, per):
    h, d, _ = w.shape
    w4 = w.reshape(h // per, per, d, d)
    eye = jnp.eye(per, dtype=w.dtype)
    return jnp.einsum("cpij,pq->cpiqj", w4, eye).reshape(h // per, per * d, per * d)


def _mixer(proj, conv_a, conv_b, conv_b_bias, rg_w_a, rg_b_a, rg_w_x, rg_b_x,
           rg_lambda, *, ts=512, cb=128, tc=32):
    seq, bsz, _ = proj.shape
    d_a = conv_a.shape[1]
    d_b = conv_b.shape[1]
    assert d_a == d_b
    ts = _tile(seq, ts)
    nc = d_a // cb
    per = cb // _B_HEAD_DIM
    wa = _block_diag(rg_w_a * -_LOG2_E, per).astype(_BF16)
    wx = _block_diag(rg_w_x * -_LOG2_E, per).astype(_BF16)
    rg_b_a = rg_b_a * -_LOG2_E
    rg_b_x = rg_b_x * -_LOG2_E

    def seg_spec(seg):
        return pl.BlockSpec((ts, bsz, cb), lambda c, i: (i, 0, seg * nc + c))

    row_spec = pl.BlockSpec((1, cb), lambda c, i: (0, c))
    gate_w_spec = pl.BlockSpec((None, cb, cb), lambda c, i: (c, 0, 0))
    out_spec = pl.BlockSpec((ts * bsz, cb), lambda c, i: (i, c))
    return pl.pallas_call(
        functools.partial(_mixer_kernel, tc=tc),
        grid=(nc, seq // ts),
        in_specs=[seg_spec(k) for k in range(5)] + [
            pl.BlockSpec((_CONV_A_WIDTH, 1, cb), lambda c, i: (0, 0, c)),
            pl.BlockSpec((_CONV_B_WIDTH, 1, cb), lambda c, i: (0, 0, c)),
            row_spec, gate_w_spec, row_spec, gate_w_spec, row_spec, row_spec,
        ],
        out_specs=[out_spec, out_spec],
        out_shape=[jax.ShapeDtypeStruct((seq * bsz, d_a), _BF16),
                   jax.ShapeDtypeStruct((seq * bsz, d_b), _BF16)],
        scratch_shapes=[
            pltpu.VMEM((ts + _CONV_A_WIDTH - 1, bsz, cb), _F32),
            pltpu.VMEM((ts + _CONV_B_WIDTH - 1, bsz, cb), _F32),
            pltpu.VMEM((bsz, cb), _F32),
        ],
        compiler_params=_compiler_params(2),
        name="conv_rglru_mixer",
    )(proj, proj, proj, proj, proj,
      conv_a.reshape(_CONV_A_WIDTH, 1, d_a), conv_b.reshape(_CONV_B_WIDTH, 1, d_b),
      conv_b_bias.reshape(1, d_b), wa, rg_b_a.reshape(1, d_b), wx,
      rg_b_x.reshape(1, d_b), rg_lambda.reshape(1, d_b))


def _attention_kernel(q_ref, k_ref, v_ref, o_ref, acc_ref, r_ref, *, tb,
                      first_window):
    seq = q_ref.shape[0]
    dh = _SB_HEAD_DIM
    heads = range(q_ref.shape[1] // dh)
    row = lax.broadcasted_iota(jnp.int32, (tb, tb), 0)
    col = lax.broadcasted_iota(jnp.int32, (tb, tb), 1)
    causal = col < row
    later = jnp.where(row > col, 1.0, 0.0).astype(_BF16)
    later2 = jnp.concatenate([later, later], axis=0)

    def scores(h, q0, j0, nb):
        lanes = slice(h * dh, (h + 1) * dh)
        return lax.dot_general(
            q_ref[pl.ds(q0, tb), lanes], k_ref[pl.ds(j0, nb * tb), lanes],
            (((1,), (1,)), ((), ())), preferred_element_type=_F32)

    def neg_log2_not_beta(z, nb, ends_on_diagonal):
        p = (jnp.maximum(z, 0.0)
             + jnp.log(1.0 + jnp.exp2(-jnp.abs(z))) * _LOG2_E)
        blocks = [p[:, i * tb:(i + 1) * tb] for i in range(nb)]
        if ends_on_diagonal:
            blocks[-1] = jnp.where(causal, blocks[-1], 0.0)
        return blocks

    def block_suffix(blocks):
        stacked = jnp.concatenate(blocks, axis=0)
        hi = stacked.astype(_BF16)
        lo = (stacked - hi.astype(_F32)).astype(_BF16)
        return jnp.dot(jnp.concatenate([hi, lo], axis=1), later2,
                       preferred_element_type=_F32)

    def weights(h, z, blocks, suffix, ends_on_diagonal):
        nb = len(blocks)
        nearer = r_ref[h]
        w = [None] * nb
        for i in reversed(range(nb)):
            blk = slice(i * tb, (i + 1) * tb)
            w[i] = jnp.exp2(z[:, blk] - blocks[i] - suffix[blk] - nearer)
            nearer = nearer + jnp.sum(blocks[i], axis=1, keepdims=True)
        r_ref[h] = nearer
        if ends_on_diagonal:
            w[-1] = jnp.where(causal, w[-1], 0.0)
        return jnp.concatenate(w, axis=1).astype(_BF16)

    def accumulate(h, w, j0, nb):
        lanes = slice(h * dh, (h + 1) * dh)
        acc_ref[:, lanes] += jnp.dot(
            w, v_ref[pl.ds(j0, nb * tb), lanes], preferred_element_type=_F32)

    def key_window(q0, j0, nb, ends_on_diagonal):
        z = [scores(h, q0, j0, nb) for h in heads]
        p = [neg_log2_not_beta(z[h], nb, ends_on_diagonal) for h in heads]
        suffix = [block_suffix(p[h]) for h in heads]
        w = [weights(h, z[h], p[h], suffix[h], ends_on_diagonal) for h in heads]
        for h in heads:
            accumulate(h, w[h], j0, nb)

    def start_block():
        acc_ref[...] = jnp.zeros_like(acc_ref)
        r_ref[...] = jnp.zeros_like(r_ref)

    def finish_block(q0):
        o_ref[pl.ds(q0, tb), :] = acc_ref[...].astype(o_ref.dtype)

    def query_block(qi, carry):
        q0 = pl.multiple_of(qi * tb, tb)
        start_block()
        key_window(q0, q0 - (first_window - 1) * tb, first_window, True)
        left = qi - (first_window - 1)

        def more(c):
            n, r_min = c
            return jnp.logical_and(n < left, r_min < _F32_EXP2_UNDERFLOW)

        def earlier(c):
            n, _ = c
            key_window(q0, pl.multiple_of((left - 1 - n) * tb, tb), 1, False)
            return n + 1, jnp.min(r_ref[...])

        lax.while_loop(more, earlier, (0, jnp.min(r_ref[...])))
        finish_block(q0)
        return carry

    n_q = seq // tb
    for qi in range(min(first_window - 1, n_q)):
        start_block()
        key_window(qi * tb, 0, qi + 1, True)
        finish_block(qi * tb)
    lax.fori_loop(first_window - 1, n_q, query_block, 0)


def _attention(qkv, *, bsz, seq, n_heads, tb=256, first_window=2,
               heads_per_step=4):
    dh = _SB_HEAD_DIM
    tb = _tile(seq, tb)
    hp = _tile(n_heads, heads_per_step)
    ng = n_heads // hp
    blk = (seq, hp * dh)
    return pl.pallas_call(
        functools.partial(_attention_kernel, tb=tb, first_window=first_window),
        grid=(bsz, ng),
        in_specs=[
            pl.BlockSpec(blk, lambda b, g: (b, g)),
            pl.BlockSpec(blk, lambda b, g: (b, ng + g)),
            pl.BlockSpec(blk, lambda b, g: (b, 2 * ng + g)),
        ],
        out_specs=pl.BlockSpec(blk, lambda b, g: (b, g)),
        out_shape=jax.ShapeDtypeStruct((bsz * seq, n_heads * dh), _BF16),
        scratch_shapes=[pltpu.VMEM((tb, hp * dh), _F32),
                        pltpu.VMEM((hp, tb, 1), _F32)],
        compiler_params=_compiler_params(2),
        name="stick_breaking_attention",
    )(qkv, qkv, qkv)


def kernel(x, norm_gains, hyb_w_in, hyb_conv_a, hyb_conv_b, hyb_conv_b_bias,
           hyb_rg_w_a, hyb_rg_b_a, hyb_rg_w_x, hyb_rg_b_x, hyb_rg_lambda,
           hyb_w_out, sb_w_qkv, sb_w_o, mlp_w_up, mlp_w_down):
    bsz, seq, d = x.shape
    depth = norm_gains.shape[0]
    n_heads = d // _SB_HEAD_DIM
    w_in = hyb_w_in.astype(_BF16)
    w_out = hyb_w_out.astype(_BF16)
    col_scale = jnp.concatenate([
        jnp.full((d,), _LOG2_E / math.sqrt(_SB_HEAD_DIM), _F32),
        jnp.ones((2 * d,), _F32)])
    w_qkv = (sb_w_qkv * col_scale).astype(_BF16)
    w_o = sb_w_o.astype(_BF16)
    w_up = mlp_w_up.astype(_BF16)
    w_down = mlp_w_down.astype(_BF16)
    for layer in range(depth):
        g = norm_gains[layer]
        if layer % 2 == 0:
            e = layer // 2
            proj = _prenorm_matmul(x, g[0:1], w_in, e, out_dtype=_F32,
                                   to_time_major=True)
            y_a, y_b = _mixer(proj.reshape(seq, bsz, -1), hyb_conv_a[e],
                              hyb_conv_b[e], hyb_conv_b_bias[e], hyb_rg_w_a[e],
                              hyb_rg_b_a[e], hyb_rg_w_x[e], hyb_rg_b_x[e],
                              hyb_rg_lambda[e])
            x = _matmul_postnorm([y_a, y_b], w_out, e, g[1:2], x,
                                 from_time_major=True)
        else:
            o = layer // 2
            qkv = _prenorm_matmul(x, g[0:1], w_qkv, o, out_dtype=_BF16,
                                  to_time_major=False)
            att = _attention(qkv, bsz=bsz, seq=seq, n_heads=n_heads)
            x = _matmul_postnorm([att], w_o, o, g[1:2], x, from_time_major=False)
        x = _mlp(x, g[2:3], w_up, w_down, layer, g[3:4])
    return x
```

```python
import functools
import math

import jax
import jax.numpy as jnp
from jax import lax
from jax.experimental import pallas as pl
from jax.experimental.pallas import tpu as pltpu

_NORM_EPS = 1e-6
_LRU_C = 8.0
_B_HEAD_DIM = 64
_SB_HEAD_DIM = 128
_CONV_A_WIDTH = 3
_CONV_B_WIDTH = 4
_PERM_STEPS = 32
_LOG2_E = math.log2(math.e)
_NORM_GROUP_ROWS = 8
_F32_EXP2_UNDERFLOW = 151.0

_VMEM_LIMIT_BYTES = 56 * 1024 * 1024

_BF16 = jnp.bfloat16
_F32 = jnp.float32


def _compiler_params(n_grid_axes):
    return pltpu.CompilerParams(
        dimension_semantics=("arbitrary",) * n_grid_axes,
        vmem_limit_bytes=_VMEM_LIMIT_BYTES,
    )


def _tile(dim, target):
    t = min(dim, target)
    while dim % t:
        t //= 2
    return t


def _rms_norm(x, g):
    ms = jnp.mean(x * x, axis=-1, keepdims=True)
    return x * lax.rsqrt(ms + _NORM_EPS) * g


def _row_permutation(n_t, bsz, to_time_major):
    n = n_t * bsz
    r = lax.broadcasted_iota(jnp.int32, (n, n), 0)
    c = lax.broadcasted_iota(jnp.int32, (n, n), 1)
    if to_time_major:
        src = lax.rem(r, bsz) * n_t + lax.div(r, bsz)
    else:
        src = lax.rem(r, n_t) * bsz + lax.div(r, n_t)
    return jnp.where(c == src, 1.0, 0.0).astype(_BF16)


def _prenorm_matmul_kernel(x_ref, g_ref, w_ref, o_ref, h_ref, *stage):
    def normalise():
        if not stage:
            h_ref[...] = _rms_norm(x_ref[...], g_ref[...]).astype(h_ref.dtype)
            return
        (hb_ref,) = stage
        bsz, ts, _ = x_ref.shape
        for b in range(bsz):
            hb_ref[b] = _rms_norm(x_ref[b], g_ref[...]).astype(hb_ref.dtype)
        step = _tile(ts, _PERM_STEPS)
        perm = _row_permutation(step, bsz, True)
        for s in range(ts // step):
            slab = jnp.concatenate(
                [hb_ref[b, s * step:(s + 1) * step, :] for b in range(bsz)], axis=0)
            h_ref[s * step * bsz:(s + 1) * step * bsz, :] = jnp.dot(
                perm, slab, preferred_element_type=_F32).astype(h_ref.dtype)

    def project():
        o_ref[...] = jnp.dot(
            h_ref[...], w_ref[...], preferred_element_type=_F32
        ).astype(o_ref.dtype)

    @pl.when(pl.program_id(1) == 0)
    def _():
        normalise()
        project()

    @pl.when(pl.program_id(1) > 0)
    def _():
        project()


def _prenorm_matmul(x, g, w, layer, *, out_dtype, to_time_major, tm=1024, tn=1024):
    bsz, seq, d = x.shape
    n = w.shape[2]
    m = bsz * seq
    tm, tn = _tile(m, tm), _tile(n, tn)
    n_row_tiles = m // tm

    def x_tile(i, j):
        return jnp.minimum(i + jnp.where(j > 0, 1, 0), n_row_tiles - 1)

    if to_time_major:
        ts = tm // bsz
        x_arg = x
        x_spec = pl.BlockSpec((bsz, ts, d), lambda i, j: (0, x_tile(i, j), 0))
        scratch = [pltpu.VMEM((tm, d), _BF16), pltpu.VMEM((bsz, ts, d), _BF16)]
    else:
        x_arg = x.reshape(m, d)
        x_spec = pl.BlockSpec((tm, d), lambda i, j: (x_tile(i, j), 0))
        scratch = [pltpu.VMEM((tm, d), _BF16)]
    return pl.pallas_call(
        _prenorm_matmul_kernel,
        grid=(n_row_tiles, n // tn),
        in_specs=[
            x_spec,
            pl.BlockSpec((1, d), lambda i, j: (0, 0)),
            pl.BlockSpec((None, d, tn), lambda i, j: (layer, 0, j)),
        ],
        out_specs=pl.BlockSpec((tm, tn), lambda i, j: (i, j)),
        out_shape=jax.ShapeDtypeStruct((m, n), out_dtype),
        scratch_shapes=scratch,
        compiler_params=_compiler_params(2),
        name="prenorm_matmul",
    )(x_arg, g, w)


def _matmul_postnorm_kernel(*refs, n_pairs):
    a_refs = refs[:n_pairs]
    w_refs = refs[n_pairs:2 * n_pairs]
    g_ref, res_ref, o_ref = refs[2 * n_pairs:2 * n_pairs + 3]
    stage = refs[2 * n_pairs + 3:]
    acc = None
    for k, (a_ref, w_ref) in enumerate(zip(a_refs, w_refs)):
        if not stage:
            a = a_ref[...].astype(_BF16)
        else:
            ab_ref = stage[k]
            bsz, ts, width = ab_ref.shape
            step = _tile(ts, _PERM_STEPS)
            perm = _row_permutation(step, bsz, False)
            for s in range(ts // step):
                slab = a_ref[s * step * bsz:(s + 1) * step * bsz, :].astype(_BF16)
                moved = jnp.dot(perm, slab, preferred_element_type=_F32)
                for b in range(bsz):
                    ab_ref[b, s * step:(s + 1) * step, :] = (
                        moved[b * step:(b + 1) * step].astype(_BF16))
            a = ab_ref[...].reshape(bsz * ts, width)
        part = jnp.dot(a, w_ref[...], preferred_element_type=_F32)
        acc = part if acc is None else acc + part
    normed = _rms_norm(acc, g_ref[...])
    o_ref[...] = res_ref[...] + normed.reshape(res_ref.shape)


def _matmul_postnorm(a_list, w, layer, g, res, *, from_time_major, tm=512):
    bsz, seq, d = res.shape
    m = bsz * seq
    tm = _tile(m, tm)
    widths = [a.shape[1] for a in a_list]
    assert len(set(widths)) == 1 and sum(widths) == w.shape[1]
    width = widths[0]
    whole = lambda i: (0, 0)
    if from_time_major:
        ts = tm // bsz
        res_arg = res
        res_spec = pl.BlockSpec((bsz, ts, d), lambda i: (0, i, 0))
        scratch = [pltpu.VMEM((bsz, ts, width), _BF16) for _ in a_list]
    else:
        res_arg = res.reshape(m, d)
        res_spec = pl.BlockSpec((tm, d), lambda i: (i, 0))
        scratch = []
    out = pl.pallas_call(
        functools.partial(_matmul_postnorm_kernel, n_pairs=len(a_list)),
        grid=(m // tm,),
        in_specs=(
            [pl.BlockSpec((tm, width), lambda i: (i, 0)) for _ in a_list]
            + [pl.BlockSpec((None, width, d), lambda i, k=k: (layer, k, 0))
               for k in range(len(a_list))]
            + [pl.BlockSpec((1, d), whole), res_spec]
        ),
        out_specs=res_spec,
        out_shape=jax.ShapeDtypeStruct(res_arg.shape, _F32),
        scratch_shapes=scratch,
        compiler_params=_compiler_params(1),
        name="matmul_postnorm",
    )(*a_list, *([w] * len(a_list)), g, res_arg)
    return out.reshape(bsz, seq, d)


def _mlp_kernel(x_ref, g_in_ref, wu_ref, wd_ref, g_out_ref, o_ref, h_ref):
    f = pl.program_id(1)
    last = pl.num_programs(1) - 1

    def accumulate():
        u = jnp.dot(h_ref[...], wu_ref[...], preferred_element_type=_F32)
        u = jnp.maximum(u, 0.0)
        o_ref[...] += jnp.dot((u * u).astype(_BF16), wd_ref[...],
                              preferred_element_type=_F32)

    def begin():
        h_ref[...] = _rms_norm(x_ref[...], g_in_ref[...]).astype(h_ref.dtype)
        o_ref[...] = jnp.zeros_like(o_ref)

    def finish():
        rows = _tile(o_ref.shape[0], _NORM_GROUP_ROWS)
        for c in range(o_ref.shape[0] // rows):
            grp = slice(c * rows, (c + 1) * rows)
            o_ref[grp, :] = x_ref[grp, :] + _rms_norm(o_ref[grp, :], g_out_ref[...])

    @pl.when(f == 0)
    def _():
        begin()
        accumulate()

    @pl.when(jnp.logical_and(f > 0, f < last))
    def _():
        accumulate()

    @pl.when(f == last)
    def _():
        accumulate()
        finish()


def _mlp(x, g_in, w_up, w_down, layer, g_out, *, tm=512, tf=1024):
    bsz, seq, d = x.shape
    m = bsz * seq
    d_ff = w_up.shape[2]
    tm, tf = _tile(m, tm), _tile(d_ff, tf)
    assert d_ff // tf >= 2, "the kernel's first and last hidden steps are distinct"
    return pl.pallas_call(
        _mlp_kernel,
        grid=(m // tm, d_ff // tf),
        in_specs=[
            pl.BlockSpec((tm, d), lambda i, f: (i, 0)),
            pl.BlockSpec((1, d), lambda i, f: (0, 0)),
            pl.BlockSpec((None, d, tf), lambda i, f: (layer, 0, f)),
            pl.BlockSpec((None, tf, d), lambda i, f: (layer, f, 0)),
            pl.BlockSpec((1, d), lambda i, f: (0, 0)),
        ],
        out_specs=pl.BlockSpec((tm, d), lambda i, f: (i, 0)),
        out_shape=jax.ShapeDtypeStruct((m, d), _F32),
        scratch_shapes=[pltpu.VMEM((tm, d), _BF16)],
        compiler_params=_compiler_params(2),
        name="sq_relu_mlp",
    )(x.reshape(m, d), g_in, w_up, w_down, g_out).reshape(bsz, seq, d)


def _mixer_kernel(bg_ref, cg_ref, ax_ref, gate_ref, bx_ref,
                  ca_ref, cb_ref, cbb_ref,
                  wa_ref, ba_ref, wx_ref, bxb_ref, lam_ref,
                  ya_ref, yb_ref,
                  u_ref, xs_ref, h_ref, *, tc):
    ts, bsz, cb = bg_ref.shape
    ka, kb = _CONV_A_WIDTH, _CONV_B_WIDTH

    @pl.when(pl.program_id(1) == 0)
    def _():
        u_ref[0:ka - 1] = jnp.zeros((ka - 1, bsz, cb), _F32)
        xs_ref[0:kb - 1] = jnp.zeros((kb - 1, bsz, cb), _F32)
        h_ref[...] = jnp.zeros((bsz, cb), _F32)

    c_log_sig = _LRU_C * jax.nn.log_sigmoid(lam_ref[...])

    def chunk(s, h):
        t0 = pl.multiple_of(s * tc, tc)
        cur = pl.ds(t0, tc)
        out_rows = pl.ds(pl.multiple_of(t0 * bsz, tc * bsz), tc * bsz)

        u_ref[pl.ds(t0 + ka - 1, tc)] = cg_ref[cur] * ax_ref[cur]
        conv = ca_ref[ka - 1] * u_ref[pl.ds(t0 + ka - 1, tc)]
        for k in range(ka - 1):
            conv = conv + ca_ref[k] * u_ref[pl.ds(t0 + k, tc)]
        ya_ref[out_rows] = (bg_ref[cur] * conv).reshape(tc * bsz, cb).astype(
            ya_ref.dtype)

        xs_ref[pl.ds(t0 + kb - 1, tc)] = bx_ref[cur]
        xr = cb_ref[kb - 1] * xs_ref[pl.ds(t0 + kb - 1, tc)]
        for k in range(kb - 1):
            xr = xr + cb_ref[k] * xs_ref[pl.ds(t0 + k, tc)]
        xr = (xr + cbb_ref[...]).reshape(tc * bsz, cb)
        xr_bf = xr.astype(_BF16)
        r = 1.0 / (1.0 + jnp.exp2(
            jnp.dot(xr_bf, wa_ref[...], preferred_element_type=_F32) + ba_ref[...]))
        gate_i = 1.0 / (1.0 + jnp.exp2(
            jnp.dot(xr_bf, wx_ref[...], preferred_element_type=_F32) + bxb_ref[...]))
        log_a = c_log_sig * r
        a = jnp.exp(log_a).reshape(tc, bsz, cb)
        tanh_log_a = jnp.tanh(log_a)
        mult = jnp.sqrt(-2.0 * tanh_log_a / (1.0 - tanh_log_a))
        b = (mult * (gate_i * xr)).reshape(tc, bsz, cb)
        gelu_gate = jax.nn.gelu(gate_ref[cur], approximate=True)
        states = []
        for t in range(tc):
            h = a[t] * h + b[t]
            states.append(h)
        yb_ref[out_rows] = (jnp.stack(states) * gelu_gate).reshape(
            tc * bsz, cb).astype(yb_ref.dtype)
        return h

    h_ref[...] = lax.fori_loop(0, ts // tc, chunk, h_ref[...])
    u_ref[0:ka - 1] = u_ref[ts:ts + ka - 1]
    xs_ref[0:kb - 1] = xs_ref[ts:ts + kb - 1]


def _block_diag(w, per):
    h, d, _ = w.shape
    w4 = w.reshape(h // per, per, d, d)
    eye = jnp.eye(per, dtype=w.dtype)
    return jnp.einsum("cpij,pq->cpiqj", w4, eye).reshape(h // per, per * d, per * d)


def _mixer(proj, conv_a, conv_b, conv_b_bias, rg_w_a, rg_b_a, rg_w_x, rg_b_x,
           rg_lambda, *, ts=512, cb=128, tc=32):
    seq, bsz, _ = proj.shape
    d_a = conv_a.shape[1]
    d_b = conv_b.shape[1]
    assert d_a == d_b
    ts = _tile(seq, ts)
    nc = d_a // cb
    per = cb // _B_HEAD_DIM
    wa = _block_diag(rg_w_a * -_LOG2_E, per).astype(_BF16)
    wx = _block_diag(rg_w_x * -_LOG2_E, per).astype(_BF16)
    rg_b_a = rg_b_a * -_LOG2_E
    rg_b_x = rg_b_x * -_LOG2_E

    def seg_spec(seg):
        return pl.BlockSpec((ts, bsz, cb), lambda c, i: (i, 0, seg * nc + c))

    row_spec = pl.BlockSpec((1, cb), lambda c, i: (0, c))
    gate_w_spec = pl.BlockSpec((None, cb, cb), lambda c, i: (c, 0, 0))
    out_spec = pl.BlockSpec((ts * bsz, cb), lambda c, i: (i, c))
    return pl.pallas_call(
        functools.partial(_mixer_kernel, tc=tc),
        grid=(nc, seq // ts),
        in_specs=[seg_spec(k) for k in range(5)] + [
            pl.BlockSpec((_CONV_A_WIDTH, 1, cb), lambda c, i: (0, 0, c)),
            pl.BlockSpec((_CONV_B_WIDTH, 1, cb), lambda c, i: (0, 0, c)),
            row_spec, gate_w_spec, row_spec, gate_w_spec, row_spec, row_spec,
        ],
        out_specs=[out_spec, out_spec],
        out_shape=[jax.ShapeDtypeStruct((seq * bsz, d_a), _BF16),
                   jax.ShapeDtypeStruct((seq * bsz, d_b), _BF16)],
        scratch_shapes=[
            pltpu.VMEM((ts + _CONV_A_WIDTH - 1, bsz, cb), _F32),
            pltpu.VMEM((ts + _CONV_B_WIDTH - 1, bsz, cb), _F32),
            pltpu.VMEM((bsz, cb), _F32),
        ],
        compiler_params=_compiler_params(2),
        name="conv_rglru_mixer",
    )(proj, proj, proj, proj, proj,
      conv_a.reshape(_CONV_A_WIDTH, 1, d_a), conv_b.reshape(_CONV_B_WIDTH, 1, d_b),
      conv_b_bias.reshape(1, d_b), wa, rg_b_a.reshape(1, d_b), wx,
      rg_b_x.reshape(1, d_b), rg_lambda.reshape(1, d_b))


def _attention_kernel(q_ref, k_ref, v_ref, o_ref, acc_ref, r_ref, *, tb,
                      first_window):
    seq = q_ref.shape[0]
    dh = _SB_HEAD_DIM
    heads = range(q_ref.shape[1] // dh)
    row = lax.broadcasted_iota(jnp.int32, (tb, tb), 0)
    col = lax.broadcasted_iota(jnp.int32, (tb, tb), 1)
    causal = col < row
    later = jnp.where(row > col, 1.0, 0.0).astype(_BF16)
    later2 = jnp.concatenate([later, later], axis=0)

    def scores(h, q0, j0, nb):
        lanes = slice(h * dh, (h + 1) * dh)
        return lax.dot_general(
            q_ref[pl.ds(q0, tb), lanes], k_ref[pl.ds(j0, nb * tb), lanes],
            (((1,), (1,)), ((), ())), preferred_element_type=_F32)

    def neg_log2_not_beta(z, nb, ends_on_diagonal):
        p = (jnp.maximum(z, 0.0)
             + jnp.log(1.0 + jnp.exp2(-jnp.abs(z))) * _LOG2_E)
        blocks = [p[:, i * tb:(i + 1) * tb] for i in range(nb)]
        if ends_on_diagonal:
            blocks[-1] = jnp.where(causal, blocks[-1], 0.0)
        return blocks

    def block_suffix(blocks):
        stacked = jnp.concatenate(blocks, axis=0)
        hi = stacked.astype(_BF16)
        lo = (stacked - hi.astype(_F32)).astype(_BF16)
        return jnp.dot(jnp.concatenate([hi, lo], axis=1), later2,
                       preferred_element_type=_F32)

    def weights(h, z, blocks, suffix, ends_on_diagonal):
        nb = len(blocks)
        nearer = r_ref[h]
        w = [None] * nb
        for i in reversed(range(nb)):
            blk = slice(i * tb, (i + 1) * tb)
            w[i] = jnp.exp2(z[:, blk] - blocks[i] - suffix[blk] - nearer)
            nearer = nearer + jnp.sum(blocks[i], axis=1, keepdims=True)
        r_ref[h] = nearer
        if ends_on_diagonal:
            w[-1] = jnp.where(causal, w[-1], 0.0)
        return jnp.concatenate(w, axis=1).astype(_BF16)

    def accumulate(h, w, j0, nb):
        lanes = slice(h * dh, (h + 1) * dh)
        acc_ref[:, lanes] += jnp.dot(
            w, v_ref[pl.ds(j0, nb * tb), lanes], preferred_element_type=_F32)

    def key_window(q0, j0, nb, ends_on_diagonal):
        z = [scores(h, q0, j0, nb) for h in heads]
        p = [neg_log2_not_beta(z[h], nb, ends_on_diagonal) for h in heads]
        suffix = [block_suffix(p[h]) for h in heads]
        w = [weights(h, z[h], p[h], suffix[h], ends_on_diagonal) for h in heads]
        for h in heads:
            accumulate(h, w[h], j0, nb)

    def start_block():
        acc_ref[...] = jnp.zeros_like(acc_ref)
        r_ref[...] = jnp.zeros_like(r_ref)

    def finish_block(q0):
        o_ref[pl.ds(q0, tb), :] = acc_ref[...].astype(o_ref.dtype)

    def query_block(qi, carry):
        q0 = pl.multiple_of(qi * tb, tb)
        start_block()
        key_window(q0, q0 - (first_window - 1) * tb, first_window, True)
        left = qi - (first_window - 1)

        def more(c):
            n, r_min = c
            return jnp.logical_and(n < left, r_min < _F32_EXP2_UNDERFLOW)

        def earlier(c):
            n, _ = c
            key_window(q0, pl.multiple_of((left - 1 - n) * tb, tb), 1, False)
            return n + 1, jnp.min(r_ref[...])

        lax.while_loop(more, earlier, (0, jnp.min(r_ref[...])))
        finish_block(q0)
        return carry

    n_q = seq // tb
    for qi in range(min(first_window - 1, n_q)):
        start_block()
        key_window(qi * tb, 0, qi + 1, True)
        finish_block(qi * tb)
    lax.fori_loop(first_window - 1, n_q, query_block, 0)


def _attention(qkv, *, bsz, seq, n_heads, tb=256, first_window=2,
               heads_per_step=4):
    dh = _SB_HEAD_DIM
    tb = _tile(seq, tb)
    hp = _tile(n_heads, heads_per_step)
    ng = n_heads // hp
    blk = (seq, hp * dh)
    return pl.pallas_call(
        functools.partial(_attention_kernel, tb=tb, first_window=first_window),
        grid=(bsz, ng),
        in_specs=[
            pl.BlockSpec(blk, lambda b, g: (b, g)),
            pl.BlockSpec(blk, lambda b, g: (b, ng + g)),
            pl.BlockSpec(blk, lambda b, g: (b, 2 * ng + g)),
        ],
        out_specs=pl.BlockSpec(blk, lambda b, g: (b, g)),
        out_shape=jax.ShapeDtypeStruct((bsz * seq, n_heads * dh), _BF16),
        scratch_shapes=[pltpu.VMEM((tb, hp * dh), _F32),
                        pltpu.VMEM((hp, tb, 1), _F32)],
        compiler_params=_compiler_params(2),
        name="stick_breaking_attention",
    )(qkv, qkv, qkv)


def kernel(x, norm_gains, hyb_w_in, hyb_conv_a, hyb_conv_b, hyb_conv_b_bias,
           hyb_rg_w_a, hyb_rg_b_a, hyb_rg_w_x, hyb_rg_b_x, hyb_rg_lambda,
           hyb_w_out, sb_w_qkv, sb_w_o, mlp_w_up, mlp_w_down):
    bsz, seq, d = x.shape
    depth = norm_gains.shape[0]
    n_heads = d // _SB_HEAD_DIM
    w_in = hyb_w_in.astype(_BF16)
    w_out = hyb_w_out.astype(_BF16)
    col_scale = jnp.concatenate([
        jnp.full((d,), _LOG2_E / math.sqrt(_SB_HEAD_DIM), _F32),
        jnp.ones((2 * d,), _F32)])
    w_qkv = (sb_w_qkv * col_scale).astype(_BF16)
    w_o = sb_w_o.astype(_BF16)
    w_up = mlp_w_up.astype(_BF16)
    w_down = mlp_w_down.astype(_BF16)
    for layer in range(depth):
        g = norm_gains[layer]
        if layer % 2 == 0:
            e = layer // 2
            proj = _prenorm_matmul(x, g[0:1], w_in, e, out_dtype=_F32,
                                   to_time_major=True, tn=1280)
            y_a, y_b = _mixer(proj.reshape(seq, bsz, -1), hyb_conv_a[e],
                              hyb_conv_b[e], hyb_conv_b_bias[e], hyb_rg_w_a[e],
                              hyb_rg_b_a[e], hyb_rg_w_x[e], hyb_rg_b_x[e],
                              hyb_rg_lambda[e])
            x = _matmul_postnorm([y_a, y_b], w_out, e, g[1:2], x,
                                 from_time_major=True)
        else:
            o = layer // 2
            qkv = _prenorm_matmul(x, g[0:1], w_qkv, o, out_dtype=_BF16,
                                  to_time_major=False, tn=2048)
            att = _attention(qkv, bsz=bsz, seq=seq, n_heads=n_heads)
            x = _matmul_postnorm([att], w_o, o, g[1:2], x, from_time_major=False)
        x = _mlp(x, g[2:3], w_up, w_down, layer, g[3:4])
    return x
```

```python
import functools
import math

import jax
import jax.numpy as jnp
from jax import lax
from jax.experimental import pallas as pl
from jax.experimental.pallas import tpu as pltpu

_NORM_EPS = 1e-6
_LRU_C = 8.0
_B_HEAD_DIM = 64
_SB_HEAD_DIM = 128
_CONV_A_WIDTH = 3
_CONV_B_WIDTH = 4
_PERM_STEPS = 32
_LOG2_E = math.log2(math.e)
_NORM_GROUP_ROWS = 8
_F32_EXP2_UNDERFLOW = 151.0

_VMEM_LIMIT_BYTES = 56 * 1024 * 1024

_BF16 = jnp.bfloat16
_F32 = jnp.float32


def _compiler_params(n_grid_axes):
    return pltpu.CompilerParams(
        dimension_semantics=("arbitrary",) * n_grid_axes,
        vmem_limit_bytes=_VMEM_LIMIT_BYTES,
    )


def _tile(dim, target):
    t = min(dim, target)
    while dim % t:
        t //= 2
    return t


def _rms_norm(x, g):
    ms = jnp.mean(x * x, axis=-1, keepdims=True)
    return x * lax.rsqrt(ms + _NORM_EPS) * g


def _row_permutation(n_t, bsz, to_time_major):
    n = n_t * bsz
    r = lax.broadcasted_iota(jnp.int32, (n, n), 0)
    c = lax.broadcasted_iota(jnp.int32, (n, n), 1)
    if to_time_major:
        src = lax.rem(r, bsz) * n_t + lax.div(r, bsz)
    else:
        src = lax.rem(r, n_t) * bsz + lax.div(r, n_t)
    return jnp.where(c == src, 1.0, 0.0).astype(_BF16)


def _prenorm_matmul_kernel(x_ref, g_ref, w_ref, o_ref, h_ref, *stage):
    def normalise():
        if not stage:
            h_ref[...] = _rms_norm(x_ref[...], g_ref[...]).astype(h_ref.dtype)
            return
        (hb_ref,) = stage
        bsz, ts, _ = x_ref.shape
        for b in range(bsz):
            hb_ref[b] = _rms_norm(x_ref[b], g_ref[...]).astype(hb_ref.dtype)
        step = _tile(ts, _PERM_STEPS)
        perm = _row_permutation(step, bsz, True)
        for s in range(ts // step):
            slab = jnp.concatenate(
                [hb_ref[b, s * step:(s + 1) * step, :] for b in range(bsz)], axis=0)
            h_ref[s * step * bsz:(s + 1) * step * bsz, :] = jnp.dot(
                perm, slab, preferred_element_type=_F32).astype(h_ref.dtype)

    def project():
        o_ref[...] = jnp.dot(
            h_ref[...], w_ref[...], preferred_element_type=_F32
        ).astype(o_ref.dtype)

    @pl.when(pl.program_id(1) == 0)
    def _():
        normalise()
        project()

    @pl.when(pl.program_id(1) > 0)
    def _():
        project()


def _prenorm_matmul(x, g, w, layer, *, out_dtype, to_time_major, tm=1024, tn=1024):
    bsz, seq, d = x.shape
    n = w.shape[2]
    m = bsz * seq
    tm, tn = _tile(m, tm), _tile(n, tn)
    n_row_tiles = m // tm

    def x_tile(i, j):
        return jnp.minimum(i + jnp.where(j > 0, 1, 0), n_row_tiles - 1)

    if to_time_major:
        ts = tm // bsz
        x_arg = x
        x_spec = pl.BlockSpec((bsz, ts, d), lambda i, j: (0, x_tile(i, j), 0))
        scratch = [pltpu.VMEM((tm, d), _BF16), pltpu.VMEM((bsz, ts, d), _BF16)]
    else:
        x_arg = x.reshape(m, d)
        x_spec = pl.BlockSpec((tm, d), lambda i, j: (x_tile(i, j), 0))
        scratch = [pltpu.VMEM((tm, d), _BF16)]
    return pl.pallas_call(
        _prenorm_matmul_kernel,
        grid=(n_row_tiles, n // tn),
        in_specs=[
            x_spec,
            pl.BlockSpec((1, d), lambda i, j: (0, 0)),
            pl.BlockSpec((None, d, tn), lambda i, j: (layer, 0, j)),
        ],
        out_specs=pl.BlockSpec((tm, tn), lambda i, j: (i, j)),
        out_shape=jax.ShapeDtypeStruct((m, n), out_dtype),
        scratch_shapes=scratch,
        compiler_params=_compiler_params(2),
        name="prenorm_matmul",
    )(x_arg, g, w)


def _matmul_postnorm_kernel(*refs, n_pairs):
    a_refs = refs[:n_pairs]
    w_refs = refs[n_pairs:2 * n_pairs]
    g_ref, res_ref, o_ref = refs[2 * n_pairs:2 * n_pairs + 3]
    stage = refs[2 * n_pairs + 3:]
    acc = None
    for k, (a_ref, w_ref) in enumerate(zip(a_refs, w_refs)):
        if not stage:
            a = a_ref[...].astype(_BF16)
        else:
            ab_ref = stage[k]
            bsz, ts, width = ab_ref.shape
            step = _tile(ts, _PERM_STEPS)
            perm = _row_permutation(step, bsz, False)
            for s in range(ts // step):
                slab = a_ref[s * step * bsz:(s + 1) * step * bsz, :].astype(_BF16)
                moved = jnp.dot(perm, slab, preferred_element_type=_F32)
                for b in range(bsz):
                    ab_ref[b, s * step:(s + 1) * step, :] = (
                        moved[b * step:(b + 1) * step].astype(_BF16))
            a = ab_ref[...].reshape(bsz * ts, width)
        part = jnp.dot(a, w_ref[...], preferred_element_type=_F32)
        acc = part if acc is None else acc + part
    normed = _rms_norm(acc, g_ref[...])
    o_ref[...] = res_ref[...] + normed.reshape(res_ref.shape)


def _matmul_postnorm(a_list, w, layer, g, res, *, from_time_major, tm=512):
    bsz, seq, d = res.shape
    m = bsz * seq
    tm = _tile(m, tm)
    widths = [a.shape[1] for a in a_list]
    assert len(set(widths)) == 1 and sum(widths) == w.shape[1]
    width = widths[0]
    whole = lambda i: (0, 0)
    if from_time_major:
        ts = tm // bsz
        res_arg = res
        res_spec = pl.BlockSpec((bsz, ts, d), lambda i: (0, i, 0))
        scratch = [pltpu.VMEM((bsz, ts, width), _BF16) for _ in a_list]
    else:
        res_arg = res.reshape(m, d)
        res_spec = pl.BlockSpec((tm, d), lambda i: (i, 0))
        scratch = []
    out = pl.pallas_call(
        functools.partial(_matmul_postnorm_kernel, n_pairs=len(a_list)),
        grid=(m // tm,),
        in_specs=(
            [pl.BlockSpec((tm, width), lambda i: (i, 0)) for _ in a_list]
            + [pl.BlockSpec((None, width, d), lambda i, k=k: (layer, k, 0))
               for k in range(len(a_list))]
            + [pl.BlockSpec((1, d), whole), res_spec]
        ),
        out_specs=res_spec,
        out_shape=jax.ShapeDtypeStruct(res_arg.shape, _F32),
        scratch_shapes=scratch,
        compiler_params=_compiler_params(1),
        name="matmul_postnorm",
    )(*a_list, *([w] * len(a_list)), g, res_arg)
    return out.reshape(bsz, seq, d)


def _mlp_kernel(x_ref, g_in_ref, wu_ref, wd_ref, g_out_ref, o_ref, h_ref):
    f = pl.program_id(1)
    last = pl.num_programs(1) - 1

    def accumulate():
        u = jnp.dot(h_ref[...], wu_ref[...], preferred_element_type=_F32)
        u = jnp.maximum(u, 0.0)
        o_ref[...] += jnp.dot((u * u).astype(_BF16), wd_ref[...],
                              preferred_element_type=_F32)

    def begin():
        h_ref[...] = _rms_norm(x_ref[...], g_in_ref[...]).astype(h_ref.dtype)
        o_ref[...] = jnp.zeros_like(o_ref)

    def finish():
        rows = _tile(o_ref.shape[0], _NORM_GROUP_ROWS)
        for c in range(o_ref.shape[0] // rows):
            grp = slice(c * rows, (c + 1) * rows)
            o_ref[grp, :] = x_ref[grp, :] + _rms_norm(o_ref[grp, :], g_out_ref[...])

    @pl.when(f == 0)
    def _():
        begin()
        accumulate()

    @pl.when(jnp.logical_and(f > 0, f < last))
    def _():
        accumulate()

    @pl.when(f == last)
    def _():
        accumulate()
        finish()


def _mlp(x, g_in, w_up, w_down, layer, g_out, *, tm=512, tf=1024):
    bsz, seq, d = x.shape
    m = bsz * seq
    d_ff = w_up.shape[2]
    tm, tf = _tile(m, tm), _tile(d_ff, tf)
    assert d_ff // tf >= 2, "the kernel's first and last hidden steps are distinct"
    return pl.pallas_call(
        _mlp_kernel,
        grid=(m // tm, d_ff // tf),
        in_specs=[
            pl.BlockSpec((tm, d), lambda i, f: (i, 0)),
            pl.BlockSpec((1, d), lambda i, f: (0, 0)),
            pl.BlockSpec((None, d, tf), lambda i, f: (layer, 0, f)),
            pl.BlockSpec((None, tf, d), lambda i, f: (layer, f, 0)),
            pl.BlockSpec((1, d), lambda i, f: (0, 0)),
        ],
        out_specs=pl.BlockSpec((tm, d), lambda i, f: (i, 0)),
        out_shape=jax.ShapeDtypeStruct((m, d), _F32),
        scratch_shapes=[pltpu.VMEM((tm, d), _BF16)],
        compiler_params=_compiler_params(2),
        name="sq_relu_mlp",
    )(x.reshape(m, d), g_in, w_up, w_down, g_out).reshape(bsz, seq, d)


def _mixer_kernel(bg_ref, cg_ref, ax_ref, gate_ref, bx_ref,
                  ca_ref, cb_ref, cbb_ref,
                  wa_ref, ba_ref, wx_ref, bxb_ref, lam_ref,
                  ya_ref, yb_ref,
                  u_ref, xs_ref, h_ref, *, tc):
    ts, bsz, cb = bg_ref.shape
    ka, kb = _CONV_A_WIDTH, _CONV_B_WIDTH

    @pl.when(pl.program_id(1) == 0)
    def _():
        u_ref[0:ka - 1] = jnp.zeros((ka - 1, bsz, cb), _F32)
        xs_ref[0:kb - 1] = jnp.zeros((kb - 1, bsz, cb), _F32)
        h_ref[...] = jnp.zeros((bsz, cb), _F32)

    c_log_sig = _LRU_C * jax.nn.log_sigmoid(lam_ref[...])

    def chunk(s, h):
        t0 = pl.multiple_of(s * tc, tc)
        cur = pl.ds(t0, tc)
        out_rows = pl.ds(pl.multiple_of(t0 * bsz, tc * bsz), tc * bsz)

        u_ref[pl.ds(t0 + ka - 1, tc)] = cg_ref[cur] * ax_ref[cur]
        conv = ca_ref[ka - 1] * u_ref[pl.ds(t0 + ka - 1, tc)]
        for k in range(ka - 1):
            conv = conv + ca_ref[k] * u_ref[pl.ds(t0 + k, tc)]
        ya_ref[out_rows] = (bg_ref[cur] * conv).reshape(tc * bsz, cb).astype(
            ya_ref.dtype)

        xs_ref[pl.ds(t0 + kb - 1, tc)] = bx_ref[cur]
        xr = cb_ref[kb - 1] * xs_ref[pl.ds(t0 + kb - 1, tc)]
        for k in range(kb - 1):
            xr = xr + cb_ref[k] * xs_ref[pl.ds(t0 + k, tc)]
        xr = (xr + cbb_ref[...]).reshape(tc * bsz, cb)
        xr_bf = xr.astype(_BF16)
        r = 1.0 / (1.0 + jnp.exp2(
            jnp.dot(xr_bf, wa_ref[...], preferred_element_type=_F32) + ba_ref[...]))
        gate_i = 1.0 / (1.0 + jnp.exp2(
            jnp.dot(xr_bf, wx_ref[...], preferred_element_type=_F32) + bxb_ref[...]))
        log_a = c_log_sig * r
        a = jnp.exp(log_a).reshape(tc, bsz, cb)
        tanh_log_a = jnp.tanh(log_a)
        mult = jnp.sqrt(-2.0 * tanh_log_a / (1.0 - tanh_log_a))
        b = (mult * (gate_i * xr)).reshape(tc, bsz, cb)
        gelu_gate = jax.nn.gelu(gate_ref[cur], approximate=True)
        states = []
        for t in range(tc):
            h = a[t] * h + b[t]
            states.append(h)
        yb_ref[out_rows] = (jnp.stack(states) * gelu_gate).reshape(
            tc * bsz, cb).astype(yb_ref.dtype)
        return h

    h_ref[...] = lax.fori_loop(0, ts // tc, chunk, h_ref[...])
    u_ref[0:ka - 1] = u_ref[ts:ts + ka - 1]
    xs_ref[0:kb - 1] = xs_ref[ts:ts + kb - 1]


def _block_diag(w, per):
    h, d, _ = w.shape
    w4 = w.reshape(h // per, per, d, d)
    eye = jnp.eye(per, dtype=w.dtype)
    return jnp.einsum("cpij,pq->cpiqj", w4, eye).reshape(h // per, per * d, per * d)


def _mixer(proj, conv_a, conv_b, conv_b_bias, rg_w_a, rg_b_a, rg_w_x, rg_b_x,
           rg_lambda, *, ts=512, cb=128, tc=64):
    seq, bsz, _ = proj.shape
    d_a = conv_a.shape[1]
    d_b = conv_b.shape[1]
    assert d_a == d_b
    ts = _tile(seq, ts)
    nc = d_a // cb
    per = cb // _B_HEAD_DIM
    wa = _block_diag(rg_w_a * -_LOG2_E, per).astype(_BF16)
    wx = _block_diag(rg_w_x * -_LOG2_E, per).astype(_BF16)
    rg_b_a = rg_b_a * -_LOG2_E
    rg_b_x = rg_b_x * -_LOG2_E

    def seg_spec(seg):
        return pl.BlockSpec((ts, bsz, cb), lambda c, i: (i, 0, seg * nc + c))

    row_spec = pl.BlockSpec((1, cb), lambda c, i: (0, c))
    gate_w_spec = pl.BlockSpec((None, cb, cb), lambda c, i: (c, 0, 0))
    out_spec = pl.BlockSpec((ts * bsz, cb), lambda c, i: (i, c))
    return pl.pallas_call(
        functools.partial(_mixer_kernel, tc=tc),
        grid=(nc, seq // ts),
        in_specs=[seg_spec(k) for k in range(5)] + [
            pl.BlockSpec((_CONV_A_WIDTH, 1, cb), lambda c, i: (0, 0, c)),
            pl.BlockSpec((_CONV_B_WIDTH, 1, cb), lambda c, i: (0, 0, c)),
            row_spec, gate_w_spec, row_spec, gate_w_spec, row_spec, row_spec,
        ],
        out_specs=[out_spec, out_spec],
        out_shape=[jax.ShapeDtypeStruct((seq * bsz, d_a), _BF16),
                   jax.ShapeDtypeStruct((seq * bsz, d_b), _BF16)],
        scratch_shapes=[
            pltpu.VMEM((ts + _CONV_A_WIDTH - 1, bsz, cb), _F32),
            pltpu.VMEM((ts + _CONV_B_WIDTH - 1, bsz, cb), _F32),
            pltpu.VMEM((bsz, cb), _F32),
        ],
        compiler_params=_compiler_params(2),
        name="conv_rglru_mixer",
    )(proj, proj, proj, proj, proj,
      conv_a.reshape(_CONV_A_WIDTH, 1, d_a), conv_b.reshape(_CONV_B_WIDTH, 1, d_b),
      conv_b_bias.reshape(1, d_b), wa, rg_b_a.reshape(1, d_b), wx,
      rg_b_x.reshape(1, d_b), rg_lambda.reshape(1, d_b))


def _attention_kernel(q_ref, k_ref, v_ref, o_ref, acc_ref, r_ref, *, tb,
                      first_window):
    seq = q_ref.shape[0]
    dh = _SB_HEAD_DIM
    heads = range(q_ref.shape[1] // dh)
    row = lax.broadcasted_iota(jnp.int32, (tb, tb), 0)
    col = lax.broadcasted_iota(jnp.int32, (tb, tb), 1)
    causal = col < row
    later = jnp.where(row > col, 1.0, 0.0).astype(_BF16)
    later2 = jnp.concatenate([later, later], axis=0)

    def scores(h, q0, j0, nb):
        lanes = slice(h * dh, (h + 1) * dh)
        return lax.dot_general(
            q_ref[pl.ds(q0, tb), lanes], k_ref[pl.ds(j0, nb * tb), lanes],
            (((1,), (1,)), ((), ())), preferred_element_type=_F32)

    def neg_log2_not_beta(z, nb, ends_on_diagonal):
        p = (jnp.maximum(z, 0.0)
             + jnp.log(1.0 + jnp.exp2(-jnp.abs(z))) * _LOG2_E)
        blocks = [p[:, i * tb:(i + 1) * tb] for i in range(nb)]
        if ends_on_diagonal:
            blocks[-1] = jnp.where(causal, blocks[-1], 0.0)
        return blocks

    def block_suffix(blocks):
        stacked = jnp.concatenate(blocks, axis=0)
        hi = stacked.astype(_BF16)
        lo = (stacked - hi.astype(_F32)).astype(_BF16)
        return jnp.dot(jnp.concatenate([hi, lo], axis=1), later2,
                       preferred_element_type=_F32)

    def weights(h, z, blocks, suffix, ends_on_diagonal):
        nb = len(blocks)
        nearer = r_ref[h]
        w = [None] * nb
        for i in reversed(range(nb)):
            blk = slice(i * tb, (i + 1) * tb)
            w[i] = jnp.exp2(z[:, blk] - blocks[i] - suffix[blk] - nearer)
            nearer = nearer + jnp.sum(blocks[i], axis=1, keepdims=True)
        r_ref[h] = nearer
        if ends_on_diagonal:
            w[-1] = jnp.where(causal, w[-1], 0.0)
        return jnp.concatenate(w, axis=1).astype(_BF16)

    def accumulate(h, w, j0, nb):
        lanes = slice(h * dh, (h + 1) * dh)
        acc_ref[:, lanes] += jnp.dot(
            w, v_ref[pl.ds(j0, nb * tb), lanes], preferred_element_type=_F32)

    def key_window(q0, j0, nb, ends_on_diagonal):
        z = [scores(h, q0, j0, nb) for h in heads]
        p = [neg_log2_not_beta(z[h], nb, ends_on_diagonal) for h in heads]
        suffix = [block_suffix(p[h]) for h in heads]
        w = [weights(h, z[h], p[h], suffix[h], ends_on_diagonal) for h in heads]
        for h in heads:
            accumulate(h, w[h], j0, nb)

    def start_block():
        acc_ref[...] = jnp.zeros_like(acc_ref)
        r_ref[...] = jnp.zeros_like(r_ref)

    def finish_block(q0):
        o_ref[pl.ds(q0, tb), :] = acc_ref[...].astype(o_ref.dtype)

    def query_block(qi, carry):
        q0 = pl.multiple_of(qi * tb, tb)
        start_block()
        key_window(q0, q0 - (first_window - 1) * tb, first_window, True)
        left = qi - (first_window - 1)

        def more(c):
            n, r_min = c
            return jnp.logical_and(n < left, r_min < _F32_EXP2_UNDERFLOW)

        def earlier(c):
            n, _ = c
            key_window(q0, pl.multiple_of((left - 1 - n) * tb, tb), 1, False)
            return n + 1, jnp.min(r_ref[...])

        lax.while_loop(more, earlier, (0, jnp.min(r_ref[...])))
        finish_block(q0)
        return carry

    n_q = seq // tb
    for qi in range(min(first_window - 1, n_q)):
        start_block()
        key_window(qi * tb, 0, qi + 1, True)
        finish_block(qi * tb)
    lax.fori_loop(first_window - 1, n_q, query_block, 0)


def _attention(qkv, *, bsz, seq, n_heads, tb=256, first_window=2,
               heads_per_step=4):
    dh = _SB_HEAD_DIM
    tb = _tile(seq, tb)
    hp = _tile(n_heads, heads_per_step)
    ng = n_heads // hp
    blk = (seq, hp * dh)
    return pl.pallas_call(
        functools.partial(_attention_kernel, tb=tb, first_window=first_window),
        grid=(bsz, ng),
        in_specs=[
            pl.BlockSpec(blk, lambda b, g: (b, g)),
            pl.BlockSpec(blk, lambda b, g: (b, ng + g)),
            pl.BlockSpec(blk, lambda b, g: (b, 2 * ng + g)),
        ],
        out_specs=pl.BlockSpec(blk, lambda b, g: (b, g)),
        out_shape=jax.ShapeDtypeStruct((bsz * seq, n_heads * dh), _BF16),
        scratch_shapes=[pltpu.VMEM((tb, hp * dh), _F32),
                        pltpu.VMEM((hp, tb, 1), _F32)],
        compiler_params=_compiler_params(2),
        name="stick_breaking_attention",
    )(qkv, qkv, qkv)


def kernel(x, norm_gains, hyb_w_in, hyb_conv_a, hyb_conv_b, hyb_conv_b_bias,
           hyb_rg_w_a, hyb_rg_b_a, hyb_rg_w_x, hyb_rg_b_x, hyb_rg_lambda,
           hyb_w_out, sb_w_qkv, sb_w_o, mlp_w_up, mlp_w_down):
    bsz, seq, d = x.shape
    depth = norm_gains.shape[0]
    n_heads = d // _SB_HEAD_DIM
    w_in = hyb_w_in.astype(_BF16)
    w_out = hyb_w_out.astype(_BF16)
    col_scale = jnp.concatenate([
        jnp.full((d,), _LOG2_E / math.sqrt(_SB_HEAD_DIM), _F32),
        jnp.ones((2 * d,), _F32)])
    w_qkv = (sb_w_qkv * col_scale).astype(_BF16)
    w_o = sb_w_o.astype(_BF16)
    w_up = mlp_w_up.astype(_BF16)
    w_down = mlp_w_down.astype(_BF16)
    for layer in range(depth):
        g = norm_gains[layer]
        if layer % 2 == 0:
            e = layer // 2
            proj = _prenorm_matmul(x, g[0:1], w_in, e, out_dtype=_F32,
                                   to_time_major=True, tn=1280)
            y_a, y_b = _mixer(proj.reshape(seq, bsz, -1), hyb_conv_a[e],
                              hyb_conv_b[e], hyb_conv_b_bias[e], hyb_rg_w_a[e],
                              hyb_rg_b_a[e], hyb_rg_w_x[e], hyb_rg_b_x[e],
                              hyb_rg_lambda[e])
            x = _matmul_postnorm([y_a, y_b], w_out, e, g[1:2], x,
                                 from_time_major=True)
        else:
            o = layer // 2
            qkv = _prenorm_matmul(x, g[0:1], w_qkv, o, out_dtype=_BF16,
                                  to_time_major=False, tn=2048)
            att = _attention(qkv, bsz=bsz, seq=seq, n_heads=n_heads)
            x = _matmul_postnorm([att], w_o, o, g[1:2], x, from_time_major=False)
        x = _mlp(x, g[2:3], w_up, w_down, layer, g[3:4])
    return x
```
